```python
import math
import jax, jax.numpy as jnp
from jax import lax
import numpy as np

D_MODEL = 1024
BATCH = 8
SEQ = 4096
DEPTH = 4
DEC_BATCH = 8
DEC_SEQ = 32
PAST_LEN = 4096

CHUNK = 64
N_MEM = 256
NORM_EPS = 1e-6
NEG_INF = -1e30
N_EVEN = (DEPTH + 1) // 2
N_ODD = DEPTH // 2
GLA_HEADS = 4
GLA_DK = 64
GLA_DV = 128
GLA_QK = GLA_HEADS * GLA_DK
GLA_V = GLA_HEADS * GLA_DV
GLA_LR = 16
GLA_TAU = 16.0
SWA_HEADS = 8
SWA_KV_HEADS = 2
SWA_GROUP = SWA_HEADS // SWA_KV_HEADS
SWA_HD = 64
WINDOW = 128
WIN_CHUNKS = WINDOW // CHUNK
N_BUCKETS = 32
MAX_DIST = 128
AB_SIZES = (GLA_QK, GLA_QK, GLA_V, GLA_V, GLA_LR, SWA_HEADS * SWA_HD, SWA_KV_HEADS * SWA_HD, SWA_KV_HEADS * SWA_HD)
AB_SPLIT = tuple(int(s) for s in np.cumsum(AB_SIZES)[:-1])
AB_IN = int(sum(AB_SIZES))
AB_OUT = GLA_V + SWA_HEADS * SWA_HD
RW_HD = 64
RW_HEADS = D_MODEL // RW_HD
RW_DECAY_LORA = 64
RW_A_LORA = 64
RW_G_LORA = 128
RW_LN_EPS = 64e-5
X_HEADS = 4
X_HD = D_MODEL // X_HEADS
D_FF = ((8 * D_MODEL + 3 * 256 - 1) // (3 * 256)) * 256

kernel_name = 'hybrid_streaming_encoder_step'

f32 = jnp.float32


def rmsnorm(x, g, eps=NORM_EPS):
    xf = x.astype(f32)
    y = xf * lax.rsqrt(jnp.mean(xf * xf, axis=-1, keepdims=True) + eps)
    return (y * g.astype(f32)).astype(x.dtype)


def t5_bias(table, lq, lk):
    rel = jnp.arange(lk)[None, :] - WINDOW - jnp.arange(lq)[:, None]
    nb = N_BUCKETS // 2
    max_exact = nb // 2
    n = jnp.abs(rel)
    nf = jnp.maximum(n, 1).astype(f32)
    large = max_exact + (jnp.log(nf / max_exact) / math.log(MAX_DIST / max_exact) * (nb - max_exact)).astype(jnp.int32)
    large = jnp.minimum(large, nb - 1)
    bucket = jnp.where(rel > 0, nb, 0) + jnp.where(n < max_exact, n, large)
    return jnp.transpose(table[bucket].astype(f32), (2, 0, 1))


def gla_chunk(S, q, k, v, loga):
    q, k, v = (t.astype(f32) for t in (q, k, v))
    L = q.shape[1]
    b = jnp.cumsum(loga, axis=1)
    causal = jnp.tril(jnp.ones((L, L), dtype=bool))[None, :, :, None, None]
    decay = jnp.exp(jnp.where(causal, b[:, :, None] - b[:, None, :], NEG_INF))
    att = jnp.einsum('bthd,bshd,btshd->bhts', q, k, decay)
    o = jnp.einsum('bhts,bshv->bthv', att, v) + jnp.einsum('bthd,bhdv->bthv', q * jnp.exp(b), S)
    b_last = b[:, -1]
    S_new = S * jnp.exp(b_last)[..., None] + jnp.einsum('bshd,bshv->bhdv', k * jnp.exp(b_last[:, None] - b), v)
    return S_new, o


def gla_scan(q, k, v, loga):
    B, T, H, _ = q.shape
    n = T // CHUNK

    def blocks(t):
        return jnp.moveaxis(t.reshape(B, n, CHUNK, H, t.shape[-1]), 1, 0)

    S0 = jnp.zeros((B, H, GLA_DK, GLA_DV), f32)
    S, o = lax.scan(lambda S, inp: gla_chunk(S, *inp), S0, (blocks(q), blocks(k), blocks(v), blocks(loga)))
    return S, jnp.moveaxis(o, 0, 1).reshape(B, T, H, GLA_DV)


def band_blocks(t):
    B, T = t.shape[:2]
    n = T // CHUNK
    tc = t.reshape(B, n, CHUNK, t.shape[2], t.shape[3])
    tp = jnp.pad(tc, ((0, 0), (WIN_CHUNKS, 0), (0, 0), (0, 0), (0, 0)))
    return jnp.concatenate([tp[:, j:j + n] for j in range(WIN_CHUNKS + 1)], axis=2)


def sink_attend(qb, kb, vb, bias, sink):
    logits = jnp.einsum('bnqkgd,bnskd->bnkgqs', qb, kb).astype(f32) * (SWA_HD ** -0.5) + bias
    s = jnp.broadcast_to(sink.astype(f32).reshape(SWA_KV_HEADS, SWA_GROUP, 1, 1), logits.shape[:-1] + (1,))
    p = jax.nn.softmax(jnp.concatenate([logits, s], axis=-1), axis=-1)[..., :-1]
    return jnp.einsum('bnkgqs,bnskd->bnqkgd', p.astype(vb.dtype), vb)


def even_mixer(h, w_in, w_a2, b_a, onorm, sink, w_out, table, gla_state, cache_k, cache_v):
    B, T, _ = h.shape
    q, k, v, g, lr, sq, sk, sv = jnp.split(h @ w_in, AB_SPLIT, axis=-1)
    q = q.reshape(B, T, GLA_HEADS, GLA_DK) * (GLA_DK ** -0.5)
    k = k.reshape(B, T, GLA_HEADS, GLA_DK)
    v = v.reshape(B, T, GLA_HEADS, GLA_DV)
    loga = (jax.nn.log_sigmoid((lr @ w_a2 + b_a).astype(f32)) / GLA_TAU).reshape(B, T, GLA_HEADS, GLA_DK)
    sq = sq.reshape(B, T, SWA_KV_HEADS, SWA_GROUP, SWA_HD)
    sk = sk.reshape(B, T, SWA_KV_HEADS, SWA_HD)
    sv = sv.reshape(B, T, SWA_KV_HEADS, SWA_HD)
    if gla_state is None:
        S, o_a = gla_scan(q, k, v, loga)
        n = T // CHUNK
        lk = WINDOW + CHUNK
        qb = sq.reshape(B, n, CHUNK, SWA_KV_HEADS, SWA_GROUP, SWA_HD)
        kb, vb = band_blocks(sk), band_blocks(sv)
        key_chunk = jnp.arange(n)[:, None] - WIN_CHUNKS + jnp.arange(lk)[None, :] // CHUNK
        mask = jnp.where(key_chunk >= 0, 0.0, NEG_INF).astype(f32)[:, None, None, None, :]
        bias = t5_bias(table, CHUNK, lk).reshape(SWA_KV_HEADS, SWA_GROUP, CHUNK, lk)[None] + mask
        new_k, new_v = sk[:, -WINDOW:], sv[:, -WINDOW:]
    else:
        S, o_a = gla_chunk(gla_state.astype(f32), q, k, v, loga)
        k_all = jnp.concatenate([cache_k.astype(sk.dtype), sk], axis=1)
        v_all = jnp.concatenate([cache_v.astype(sv.dtype), sv], axis=1)
        qb, kb, vb = sq[:, None], k_all[:, None], v_all[:, None]
        bias = t5_bias(table, T, WINDOW + T).reshape(SWA_KV_HEADS, SWA_GROUP, T, WINDOW + T)[None]
        new_k, new_v = k_all[:, -WINDOW:], v_all[:, -WINDOW:]
    o_b = sink_attend(qb, kb, vb, bias, sink).reshape(B, T, SWA_HEADS * SWA_HD)
    o_a = rmsnorm(o_a, onorm).reshape(B, T, GLA_V) * jax.nn.silu(g.astype(f32))
    out = jnp.concatenate([o_a.astype(h.dtype), o_b.astype(h.dtype)], axis=-1) @ w_out
    return out, S, new_k, new_v


def wkv7_scan(S0, r, w, k, v, a, b):
    seqs = tuple(jnp.moveaxis(t.astype(f32), 1, 0) for t in (r, w, k, v, a, b))

    def step(S, inp):
        r_t, w_t, k_t, v_t, a_t, b_t = inp
        sa = jnp.einsum('bhij,bhj->bhi', S, a_t)
        S = S * w_t[:, :, None, :] + sa[..., None] * b_t[:, :, None, :] + v_t[..., None] * k_t[:, :, None, :]
        return S, jnp.einsum('bhij,bhj->bhi', S, r_t)

    S, ys = lax.scan(step, S0, seqs)
    return S, jnp.moveaxis(ys, 0, 1)


def rwkv_mixer(h, shift, S0, mu, wr, wk, wv, wo, w0, w1, w2, a0, a1, a2, g1, g2, k_k, k_a, r_k, ln_w, ln_b):
    B, T, D = h.shape
    prev = jnp.concatenate([shift[:, None].astype(h.dtype), h[:, :-1]], axis=1)
    xs = h[None] + (prev - h)[None] * mu[:, None, None, :]
    xr, xw, xk, xv, xa, xg = xs

    def heads(t):
        return t.reshape(B, T, RW_HEADS, RW_HD)

    r = heads(xr @ wr).astype(f32)
    k = heads(xk @ wk).astype(f32)
    v = heads(xv @ wv).astype(f32)
    w = -jax.nn.softplus(-(w0 + jnp.tanh(xw @ w1) @ w2).astype(f32)) - 0.5
    decay = heads(jnp.exp(-jnp.exp(w)))
    a = heads(jax.nn.sigmoid((a0 + (xa @ a1) @ a2).astype(f32)))
    g = jax.nn.sigmoid(xg @ g1) @ g2
    kk = k * k_k.astype(f32).reshape(RW_HEADS, RW_HD)
    kk = kk / jnp.maximum(jnp.linalg.norm(kk, axis=-1, keepdims=True), 1e-12)
    k = k * (1.0 + (a - 1.0) * k_a.astype(f32).reshape(RW_HEADS, RW_HD))
    S, y = wkv7_scan(S0, r, decay, k, v, -kk, kk * a)
    mean = jnp.mean(y, axis=-1, keepdims=True)
    var = jnp.mean(jnp.square(y - mean), axis=-1, keepdims=True)
    y = ((y - mean) * lax.rsqrt(var + RW_LN_EPS)).reshape(B, T, D) * ln_w + ln_b
    y = y + (jnp.sum(r * k * r_k.astype(f32), axis=-1, keepdims=True) * v).reshape(B, T, D)
    out = (y.astype(h.dtype) * g) @ wo
    return out, S, h[:, -1]


def memory_kv(mem, g, wk, wv):
    B, M, _ = mem.shape
    m = rmsnorm(mem, g)
    return (m @ wk).reshape(B, M, X_HEADS, X_HD), (m @ wv).reshape(B, M, X_HEADS, X_HD)


def cross_attn(h, mk, mv, wq, wo):
    B, T, D = h.shape
    q = (h @ wq).reshape(B, T, X_HEADS, X_HD)
    logits = jnp.einsum('bthd,bmhd->bhtm', q, mk.astype(q.dtype)).astype(f32) * (X_HD ** -0.5)
    p = jax.nn.softmax(logits, axis=-1)
    o = jnp.einsum('bhtm,bmhd->bthd', p.astype(h.dtype), mv.astype(h.dtype)).reshape(B, T, D)
    return o @ wo


def swiglu(h, w1, w3, w2):
    return (jax.nn.silu(h @ w1) * (h @ w3)) @ w2


def setup_inputs(seed: int = 0) -> dict:
    key = jax.random.key(seed)
    ks = jax.random.split(key, 48)
    D = D_MODEL

    def nrm(i, shape, s):
        return jax.random.normal(ks[i], shape, f32) * s

    def uni(i, shape, lo, hi):
        return jax.random.uniform(ks[i], shape, f32, lo, hi)

    return {
        'x_prompt': nrm(0, (BATCH, SEQ, D), 1.0),
        'x_sample': nrm(1, (DEC_BATCH, DEC_SEQ, D), 1.0),
        'cache_swa_k': nrm(2, (N_EVEN, DEC_BATCH, WINDOW, SWA_KV_HEADS, SWA_HD), 1.0),
        'cache_swa_v': nrm(3, (N_EVEN, DEC_BATCH, WINDOW, SWA_KV_HEADS, SWA_HD), 1.0),
        'state_gla': nrm(4, (N_EVEN, DEC_BATCH, GLA_HEADS, GLA_DK, GLA_DV), 0.5),
        'state_rwkv': nrm(5, (N_ODD, DEC_BATCH, RW_HEADS, RW_HD, RW_HD), 0.5),
        'state_shift': nrm(6, (N_ODD, DEC_BATCH, D), 1.0),
        'cache_mem_k': nrm(7, (DEPTH, DEC_BATCH, N_MEM, X_HEADS, X_HD), 1.0),
        'cache_mem_v': nrm(8, (DEPTH, DEC_BATCH, N_MEM, X_HEADS, X_HD), 1.0),
        'mem_prompt': nrm(9, (BATCH, N_MEM, D), 1.0),
        'rel_bias': nrm(10, (N_BUCKETS, SWA_HEADS), 0.5),
        'norm_mix': 1.0 + nrm(11, (DEPTH, D), 0.02),
        'norm_xattn': 1.0 + nrm(12, (DEPTH, D), 0.02),
        'norm_ffn': 1.0 + nrm(13, (DEPTH, D), 0.02),
        'norm_mem': 1.0 + nrm(14, (DEPTH, D), 0.02),
        'norm_final': 1.0 + nrm(15, (D,), 0.02),
        'ab_w_in': nrm(16, (N_EVEN, D, AB_IN), D ** -0.5),
        'gla_w_a2': nrm(17, (N_EVEN, GLA_LR, GLA_QK), GLA_LR ** -0.5),
        'gla_b_a': uni(18, (N_EVEN, GLA_QK), 0.5, 3.0),
        'gla_onorm': 1.0 + nrm(19, (N_EVEN, GLA_DV), 0.02),
        'swa_sink': nrm(20, (N_EVEN, SWA_HEADS), 0.5),
        'ab_w_out': nrm(21, (N_EVEN, AB_OUT, D), AB_OUT ** -0.5),
        'rw_mu': uni(22, (N_ODD, 6, D), 0.0, 1.0),
        'rw_wr': nrm(23, (N_ODD, D, D), D ** -0.5),
        'rw_wk': nrm(24, (N_ODD, D, D), D ** -0.5),
        'rw_wv': nrm(25, (N_ODD, D, D), D ** -0.5),
        'rw_wo': nrm(26, (N_ODD, D, D), D ** -0.5),
        'rw_w0': uni(27, (N_ODD, D), -6.5, -1.5),
        'rw_w1': nrm(28, (N_ODD, D, RW_DECAY_LORA), D ** -0.5),
        'rw_w2': nrm(29, (N_ODD, RW_DECAY_LORA, D), 0.1 * RW_DECAY_LORA ** -0.5),
        'rw_a0': nrm(30, (N_ODD, D), 0.1),
        'rw_a1': nrm(31, (N_ODD, D, RW_A_LORA), D ** -0.5),
        'rw_a2': nrm(32, (N_ODD, RW_A_LORA, D), 0.1 * RW_A_LORA ** -0.5),
        'rw_g1': nrm(33, (N_ODD, D, RW_G_LORA), D ** -0.5),
        'rw_g2': nrm(34, (N_ODD, RW_G_LORA, D), RW_G_LORA ** -0.5),
        'rw_kk': 0.85 + nrm(35, (N_ODD, D), 0.05),
        'rw_ka': 1.0 + nrm(36, (N_ODD, D), 0.05),
        'rw_rk': nrm(37, (N_ODD, RW_HEADS, RW_HD), 0.1),
        'rw_lnx_w': 1.0 + nrm(38, (N_ODD, D), 0.02),
        'rw_lnx_b': nrm(39, (N_ODD, D), 0.02),
        'xa_wq': nrm(40, (DEPTH, D, D), D ** -0.5),
        'xa_wk': nrm(41, (DEPTH, D, D), D ** -0.5),
        'xa_wv': nrm(42, (DEPTH, D, D), D ** -0.5),
        'xa_wo': nrm(43, (DEPTH, D, D), D ** -0.5),
        'ffn_w1': nrm(44, (DEPTH, D, D_FF), D ** -0.5),
        'ffn_w3': nrm(45, (DEPTH, D, D_FF), D ** -0.5),
        'ffn_w2': nrm(46, (DEPTH, D_FF, D), D_FF ** -0.5),
    }


def reference(x_prompt, x_sample, cache_swa_k, cache_swa_v, state_gla, state_rwkv, state_shift, cache_mem_k, cache_mem_v, mem_prompt, rel_bias, norm_mix, norm_xattn, norm_ffn, norm_mem, norm_final, ab_w_in, gla_w_a2, gla_b_a, gla_onorm, swa_sink, ab_w_out, rw_mu, rw_wr, rw_wk, rw_wv, rw_wo, rw_w0, rw_w1, rw_w2, rw_a0, rw_a1, rw_a2, rw_g1, rw_g2, rw_kk, rw_ka, rw_rk, rw_lnx_w, rw_lnx_b, xa_wq, xa_wk, xa_wv, xa_wo, ffn_w1, ffn_w3, ffn_w2):

    def run(x, mem_k, mem_v, swa_k, swa_v, gla_s, rw_s, rw_shift):
        prompt = gla_s is None
        B = x.shape[0]
        out_k, out_v, out_gla, out_rw, out_shift = [], [], [], [], []
        for l in range(DEPTH):
            i = l // 2
            h = rmsnorm(x, norm_mix[l])
            if l % 2 == 0:
                mix, S, nk, nv = even_mixer(
                    h, ab_w_in[i], gla_w_a2[i], gla_b_a[i], gla_onorm[i], swa_sink[i], ab_w_out[i], rel_bias,
                    None if prompt else gla_s[i], None if prompt else swa_k[i], None if prompt else swa_v[i])
                out_k.append(nk)
                out_v.append(nv)
                out_gla.append(S)
            else:
                shift0 = jnp.zeros((B, D_MODEL), x.dtype) if prompt else rw_shift[i]
                S0 = jnp.zeros((B, RW_HEADS, RW_HD, RW_HD), f32) if prompt else rw_s[i].astype(f32)
                mix, S, sh = rwkv_mixer(
                    h, shift0, S0, rw_mu[i], rw_wr[i], rw_wk[i], rw_wv[i], rw_wo[i], rw_w0[i], rw_w1[i], rw_w2[i],
                    rw_a0[i], rw_a1[i], rw_a2[i], rw_g1[i], rw_g2[i], rw_kk[i], rw_ka[i], rw_rk[i],
                    rw_lnx_w[i], rw_lnx_b[i])
                out_rw.append(S)
                out_shift.append(sh)
            x = x + mix.astype(x.dtype)
            x = x + cross_attn(rmsnorm(x, norm_xattn[l]), mem_k[l], mem_v[l], xa_wq[l], xa_wo[l]).astype(x.dtype)
            x = x + swiglu(rmsnorm(x, norm_ffn[l]), ffn_w1[l], ffn_w3[l], ffn_w2[l]).astype(x.dtype)

        def stack(ts):
            return jnp.stack(ts).astype(x.dtype)

        return (rmsnorm(x, norm_final), stack(out_k), stack(out_v), stack(out_gla), stack(out_rw), stack(out_shift))

    mem_kv_list = [memory_kv(mem_prompt, norm_mem[l], xa_wk[l], xa_wv[l]) for l in range(DEPTH)]
    p_mem_k = jnp.stack([kv[0] for kv in mem_kv_list])
    p_mem_v = jnp.stack([kv[1] for kv in mem_kv_list])
    y_prompt, p_swa_k, p_swa_v, p_gla, p_rwkv, p_shift = run(x_prompt, p_mem_k, p_mem_v, None, None, None, None, None)
    y_sample, s_swa_k, s_swa_v, s_gla, s_rwkv, s_shift = run(
        x_sample, cache_mem_k, cache_mem_v, cache_swa_k, cache_swa_v, state_gla, state_rwkv, state_shift)
    return (y_prompt, y_sample, p_swa_k, p_swa_v, p_gla, p_rwkv, p_shift, p_mem_k, p_mem_v, s_swa_k, s_swa_v, s_gla, s_rwkv, s_shift)
```

```python
import functools
import math

import jax
import jax.numpy as jnp
import numpy as np
from jax import lax
from jax.experimental import pallas as pl
from jax.experimental.pallas import tpu as pltpu

F32 = jnp.float32
BF16 = jnp.bfloat16

D_MODEL = 1024
DEPTH = 4
CHUNK = 64
NORM_EPS = 1e-6
NEG_INF = -1e30
GLA_HEADS = 4
GLA_DK = 64
GLA_DV = 128
GLA_QK = GLA_HEADS * GLA_DK
GLA_V = GLA_HEADS * GLA_DV
GLA_LR = 16
GLA_TAU = 16.0
SWA_HEADS = 8
SWA_KV_HEADS = 2
SWA_GROUP = SWA_HEADS // SWA_KV_HEADS
SWA_HD = 64
WINDOW = 128
N_BUCKETS = 32
MAX_DIST = 128
RW_HD = 64
RW_HEADS = D_MODEL // RW_HD
RW_LN_EPS = 64e-5
X_HEADS = 4
X_HD = D_MODEL // X_HEADS

LANES = 128
HEAD_PAIR = 2 * RW_HD
VMEM_LIMIT = 56 * 1024 * 1024


def _cparams(*sem):
    return pltpu.CompilerParams(dimension_semantics=sem, vmem_limit_bytes=VMEM_LIMIT)


def _dot(a, b):
    return jnp.dot(a, b, preferred_element_type=F32)


def _dot_nt(a, b):
    return lax.dot_general(a, b, (((1,), (1,)), ((), ())), preferred_element_type=F32)


def _dot_tn(a, b):
    return lax.dot_general(a, b, (((0,), (0,)), ((), ())), preferred_element_type=F32)


def _rms(x, g, eps):
    return x * lax.rsqrt(jnp.mean(x * x, axis=-1, keepdims=True) + eps) * g


def _sigmoid(x):
    return 1.0 / (1.0 + jnp.exp(-x))


def _log_sigmoid(x):
    return jnp.minimum(x, 0.0) - jnp.log(1.0 + jnp.exp(-jnp.abs(x)))


def _cumsum_rows(x):
    n = x.shape[0]
    r = lax.broadcasted_iota(jnp.int32, (n, n), 0)
    c = lax.broadcasted_iota(jnp.int32, (n, n), 1)
    tri = jnp.where(c <= r, 1.0, 0.0).astype(BF16)
    hi = x.astype(BF16)
    r1 = x - hi.astype(F32)
    mid = r1.astype(BF16)
    lo = (r1 - mid.astype(F32)).astype(BF16)
    return _dot(tri, hi) + _dot(tri, mid) + _dot(tri, lo)


def _head0_mask(width, half):
    return lax.broadcasted_iota(jnp.int32, (1, width), 1) < half


def _bd(x, m0):
    zero = jnp.zeros_like(x)
    return jnp.concatenate([jnp.where(m0, x, zero), jnp.where(m0, zero, x)], axis=0)


def _seg_sum(x, ones_bd):
    parts = []
    for j in range(x.shape[1] // 256):
        parts.append(_dot(x[:, 256 * j:256 * (j + 1)].astype(BF16), ones_bd))
    return jnp.concatenate(parts, axis=1)


def _full(shape):
    nd = len(shape)
    return pl.BlockSpec(shape, lambda *_: (0,) * nd, pipeline_mode=pl.Buffered(1))


def _norm_proj_kernel(x_ref, g_ref, w_ref, *out_refs, splits, eps):
    h = _rms(x_ref[...], g_ref[...], eps).astype(BF16)
    p = _dot(h, w_ref[...])
    off = 0
    for o_ref, width in zip(out_refs, splits):
        o_ref[...] = p[:, off:off + width].astype(o_ref.dtype)
        off += width


def norm_proj(x2, g, w, splits, dtypes, tm):
    n, d = x2.shape
    ntot = w.shape[1]
    assert sum(splits) == ntot and n % tm == 0
    return pl.pallas_call(
        functools.partial(_norm_proj_kernel, splits=tuple(splits), eps=NORM_EPS),
        grid=(n // tm,),
        in_specs=[pl.BlockSpec((tm, d), lambda i: (i, 0)), _full((1, d)), _full((d, ntot))],
        out_specs=[pl.BlockSpec((tm, s), lambda i: (i, 0)) for s in splits],
        out_shape=[jax.ShapeDtypeStruct((n, s), dt) for s, dt in zip(splits, dtypes)],
        compiler_params=_cparams("parallel"),
        name="norm_proj",
    )(x2, g.reshape(1, d), w)


def _outproj_kernel(x_ref, a_ref, b_ref, wa_ref, wb_ref, o_ref):
    o_ref[...] = x_ref[...] + _dot(a_ref[...], wa_ref[...]) + _dot(b_ref[...], wb_ref[...])


def outproj_residual(x2, a, b, wa, wb, tm):
    n, d = x2.shape
    ka, kb = a.shape[1], b.shape[1]
    return pl.pallas_call(
        _outproj_kernel,
        grid=(n // tm,),
        in_specs=[pl.BlockSpec((tm, d), lambda i: (i, 0)), pl.BlockSpec((tm, ka), lambda i: (i, 0)),
                  pl.BlockSpec((tm, kb), lambda i: (i, 0)), _full((ka, d)), _full((kb, d))],
        out_specs=pl.BlockSpec((tm, d), lambda i: (i, 0)),
        out_shape=jax.ShapeDtypeStruct((n, d), F32),
        compiler_params=_cparams("parallel"),
        name="outproj_residual",
    )(x2, a, b, wa, wb)


def _xattn_kernel(x_ref, g_ref, wq_ref, mk_ref, mv_ref, wo_ref, o_ref, *, eps):
    x = x_ref[0]
    h = _rms(x, g_ref[...], eps).astype(BF16)
    q = _dot(h, wq_ref[...]).astype(BF16)
    outs = []
    for hh in range(X_HEADS):
        sl = slice(X_HD * hh, X_HD * (hh + 1))
        s = _dot_nt(q[:, sl], mk_ref[0, :, sl])
        m = jnp.max(s, axis=-1, keepdims=True)
        ex = jnp.exp(s - m)
        den = jnp.sum(ex, axis=-1, keepdims=True)
        outs.append((_dot(ex.astype(BF16), mv_ref[0, :, sl]) / den).astype(BF16))
    o = jnp.concatenate(outs, axis=1)
    o_ref[0] = x + _dot(o, wo_ref[...])


def xattn_residual(x, g, wq, mk, mv, wo, tm):
    b, t, d = x.shape
    m = mk.shape[1]
    return pl.pallas_call(
        functools.partial(_xattn_kernel, eps=NORM_EPS),
        grid=(b, t // tm),
        in_specs=[pl.BlockSpec((1, tm, d), lambda i, j: (i, j, 0)), _full((1, d)), _full((d, d)),
                  pl.BlockSpec((1, m, d), lambda i, j: (i, 0, 0)), pl.BlockSpec((1, m, d), lambda i, j: (i, 0, 0)),
                  _full((d, d))],
        out_specs=pl.BlockSpec((1, tm, d), lambda i, j: (i, j, 0)),
        out_shape=jax.ShapeDtypeStruct((b, t, d), F32),
        compiler_params=_cparams("parallel", "parallel"),
        name="xattn_residual",
    )(x, g.reshape(1, d), wq, mk, mv, wo)


def _ffn_kernel(x_ref, g_ref, w1_ref, w3_ref, w2_ref, gf_ref, o_ref, *y_ref, eps):
    x = x_ref[...]
    h = _rms(x, g_ref[...], eps).astype(BF16)
    a = _dot(h, w1_ref[...])
    b = _dot(h, w3_ref[...])
    u = (a * _sigmoid(a) * b).astype(BF16)
    o = x + _dot(u, w2_ref[...])
    o_ref[...] = o
    if y_ref:
        y_ref[0][...] = _rms(o, gf_ref[...], eps)


def ffn_residual(x2, g, w1, w3, w2, gf, tm, with_final):
    n, d = x2.shape
    f = w1.shape[1]
    tok = pl.BlockSpec((tm, d), lambda i: (i, 0))
    n_out = 2 if with_final else 1
    return pl.pallas_call(
        functools.partial(_ffn_kernel, eps=NORM_EPS),
        grid=(n // tm,),
        in_specs=[tok, _full((1, d)), _full((d, f)), _full((d, f)), _full((f, d)), _full((1, d))],
        out_specs=[tok] * n_out,
        out_shape=[jax.ShapeDtypeStruct((n, d), F32)] * n_out,
        compiler_params=_cparams("parallel"),
        name="ffn_residual",
    )(x2, g.reshape(1, d), w1, w3, w2, gf.reshape(1, d))


def _gla_kernel(q_ref, k_ref, v_ref, gate_ref, lr_ref, wa2_ref, ba_ref, on_ref, s0_ref, o_ref, sout_ref, s_scr, *, L, eps):
    c = pl.program_id(1)

    @pl.when(c == 0)
    def _():
        s_scr[...] = s0_ref[0]

    z = _dot(lr_ref[0].astype(BF16), wa2_ref[...]) + ba_ref[...]
    b = _cumsum_rows(_log_sigmoid(z) * (1.0 / GLA_TAU))
    mrow = b[L // 2:L // 2 + 1]
    blast = b[L - 1:L]
    qd = q_ref[0].astype(F32) * jnp.exp(b - mrow)
    kd = k_ref[0].astype(F32) * jnp.exp(mrow - b)
    qb = qd * jnp.exp(mrow)
    kl = kd * jnp.exp(blast - mrow)
    dec = jnp.exp(blast)
    m0 = _head0_mask(HEAD_PAIR, GLA_DK)
    m0s = _head0_mask(2 * L, L)
    rr = lax.broadcasted_iota(jnp.int32, (L, 2 * L), 0)
    cc = lax.broadcasted_iota(jnp.int32, (L, 2 * L), 1)
    causal = jnp.where(cc >= L, cc - L, cc) <= rr
    v = v_ref[0]
    gate = gate_ref[0].astype(F32)
    for p in range(GLA_HEADS // 2):
        sl = slice(HEAD_PAIR * p, HEAD_PAIR * (p + 1))
        qd_p = qd[:, sl].astype(BF16)
        kd_p = kd[:, sl].astype(BF16)
        qb_p = qb[:, sl].astype(BF16)
        kl_p = kl[:, sl].astype(BF16)
        att = _dot_nt(qd_p, _bd(kd_p, m0))
        att = jnp.where(causal, att, 0.0).astype(BF16)
        st = s_scr[p]
        st_b = st.astype(BF16)
        vst = jnp.concatenate([v[:, GLA_DV * (2 * p):GLA_DV * (2 * p + 1)],
                               v[:, GLA_DV * (2 * p + 1):GLA_DV * (2 * p + 2)]], axis=0)
        snew = st * dec[:, sl]
        for e in range(2):
            h = 2 * p + e
            hs = slice(GLA_DV * h, GLA_DV * (h + 1))
            me = m0 if e == 0 else jnp.logical_not(m0)
            mes = m0s if e == 0 else jnp.logical_not(m0s)
            o = _dot(jnp.where(mes, att, jnp.zeros_like(att)), vst)
            o = o + _dot_nt(jnp.where(me, qb_p, jnp.zeros_like(qb_p)), st_b)
            snew = snew + _dot_tn(v[:, hs], jnp.where(me, kl_p, jnp.zeros_like(kl_p)))
            o = _rms(o, on_ref[...], eps)
            g = gate[:, hs]
            o_ref[0, :, hs] = (o * (g * _sigmoid(g))).astype(o_ref.dtype)
        s_scr[p] = snew

    @pl.when(c == pl.num_programs(1) - 1)
    def _():
        sout_ref[0] = s_scr[...]


def gla_mixer(q, k, v, gate, lr, wa2p, ba, onorm, s0t, L):
    b, t, _ = q.shape
    hp = GLA_HEADS // 2

    def tok(width):
        return pl.BlockSpec((1, L, width), lambda i, j: (i, j, 0))

    st_spec = pl.BlockSpec((1, hp, GLA_DV, HEAD_PAIR), lambda i, j: (i, 0, 0, 0))
    return pl.pallas_call(
        functools.partial(_gla_kernel, L=L, eps=NORM_EPS),
        grid=(b, t // L),
        in_specs=[tok(GLA_QK), tok(GLA_QK), tok(GLA_V), tok(GLA_V), tok(LANES), _full((LANES, GLA_QK)),
                  _full((1, GLA_QK)), _full((1, GLA_DV)), st_spec],
        out_specs=[tok(GLA_V), st_spec],
        out_shape=[jax.ShapeDtypeStruct((b, t, GLA_V), BF16), jax.ShapeDtypeStruct((b, hp, GLA_DV, HEAD_PAIR), F32)],
        scratch_shapes=[pltpu.VMEM((hp, GLA_DV, HEAD_PAIR), F32)],
        compiler_params=_cparams("parallel", "arbitrary"),
        name="gla_mixer",
    )(q, k, v, gate, lr, wa2p, ba.reshape(1, GLA_QK), onorm.reshape(1, GLA_DV), s0t)


def _bias_kernel(tab_ref, bucket_ref, valid_ref, o_ref):
    bk = bucket_ref[...]
    ok = valid_ref[...] != 0
    for h in range(SWA_HEADS):
        acc = jnp.zeros(bk.shape, F32)
        for n in range(N_BUCKETS):
            acc = jnp.where(bk == n, tab_ref[n, h], acc)
        o_ref[h] = jnp.where(ok, acc, NEG_INF)


def swa_bias(table, tq, lk, chunked):
    rel = jnp.arange(lk)[None, :] - WINDOW - jnp.arange(tq)[:, None]
    nb = N_BUCKETS // 2
    max_exact = nb // 2
    n = jnp.abs(rel)
    nf = jnp.maximum(n, 1).astype(F32)
    large = max_exact + (jnp.log(nf / max_exact) / math.log(MAX_DIST / max_exact) * (nb - max_exact)).astype(jnp.int32)
    large = jnp.minimum(large, nb - 1)
    bucket = (jnp.where(rel > 0, nb, 0) + jnp.where(n < max_exact, n, large)).astype(jnp.int32)
    if chunked:
        qc = np.arange(tq)[:, None] // CHUNK
        kc = np.arange(lk)[None, :] // CHUNK
        valid = ((kc >= qc) & (kc <= qc + WINDOW // CHUNK)).astype(np.int32)
    else:
        valid = np.ones((tq, lk), np.int32)
    return pl.pallas_call(
        _bias_kernel,
        in_specs=[pl.BlockSpec(memory_space=pltpu.SMEM), _full((tq, lk)), _full((tq, lk))],
        out_specs=_full((SWA_HEADS, tq, lk)),
        out_shape=jax.ShapeDtypeStruct((SWA_HEADS, tq, lk), F32),
        grid=(1,),
        name="swa_bias",
    )(table.astype(F32), bucket, jnp.asarray(valid))


def _swa_kernel(sink_ref, q_ref, k_ref, v_ref, bias_ref, o_ref, *, tq, lk, prefix_valid):
    c = pl.program_id(1)
    start = pl.multiple_of(c * tq, tq)
    kk = k_ref[0, pl.ds(start, lk), :]
    vv = v_ref[0, pl.ds(start, lk), :]
    q = q_ref[0]
    m0 = _head0_mask(HEAD_PAIR, SWA_HD)
    if not prefix_valid:
        key_ok = (lax.broadcasted_iota(jnp.int32, (tq, lk), 1) + start) >= WINDOW
    for p in range(SWA_HEADS // 2):
        j = (2 * p) // SWA_GROUP
        qp = q[:, HEAD_PAIR * p:HEAD_PAIR * (p + 1)]
        kj = kk[:, HEAD_PAIR * j:HEAD_PAIR * (j + 1)]
        vj = vv[:, HEAD_PAIR * j:HEAD_PAIR * (j + 1)]
        outs = []
        for e in range(2):
            h = 2 * p + e
            me = m0 if e == 0 else jnp.logical_not(m0)
            s = _dot_nt(jnp.where(me, qp, jnp.zeros_like(qp)), kj) + bias_ref[h]
            if not prefix_valid:
                s = jnp.where(key_ok, s, NEG_INF)
            snk = sink_ref[h]
            m = jnp.maximum(jnp.max(s, axis=-1, keepdims=True), snk)
            ex = jnp.exp(s - m)
            den = jnp.sum(ex, axis=-1, keepdims=True) + jnp.exp(snk - m)
            outs.append(_dot(ex.astype(BF16), vj) / den)
        o_ref[0, :, HEAD_PAIR * p:HEAD_PAIR * (p + 1)] = jnp.where(m0, outs[0], outs[1]).astype(o_ref.dtype)


def swa_mixer(sq, kdup_ext, vdup_ext, bias, sink, tq, prefix_valid):
    b, t, c = sq.shape
    text = kdup_ext.shape[1]
    lk = WINDOW + tq
    return pl.pallas_call(
        functools.partial(_swa_kernel, tq=tq, lk=lk, prefix_valid=prefix_valid),
        grid=(b, t // tq),
        in_specs=[pl.BlockSpec(memory_space=pltpu.SMEM),
                  pl.BlockSpec((1, tq, c), lambda i, j: (i, j, 0)),
                  pl.BlockSpec((1, text, 2 * HEAD_PAIR), lambda i, j: (i, 0, 0)),
                  pl.BlockSpec((1, text, 2 * HEAD_PAIR), lambda i, j: (i, 0, 0)),
                  _full((SWA_HEADS, tq, lk))],
        out_specs=pl.BlockSpec((1, tq, c), lambda i, j: (i, j, 0)),
        out_shape=jax.ShapeDtypeStruct((b, t, c), BF16),
        compiler_params=_cparams("parallel", "parallel"),
        name="swa_mixer",
    )(sink.astype(F32), sq, kdup_ext, vdup_ext, bias)


def _rwkv_pre_kernel(x_ref, sh_ref, g_ref, mu_ref, wr_ref, wk_ref, wv_ref, w1_ref, w2_ref, a1_ref, a2_ref, g1_ref,
                     g2_ref, w0_ref, a0_ref, kk_ref, ka_ref, ones_ref,
                     r_o, lw_o, k_o, v_o, kn_o, ba_o, gg_o, hl_o, carry, *, tm, eps):
    t = pl.program_id(1)

    @pl.when(t == 0)
    def _():
        carry[...] = sh_ref[0]

    h = _rms(x_ref[0], g_ref[...], eps)
    row = lax.broadcasted_iota(jnp.int32, h.shape, 0)
    prev = jnp.where(row == 0, carry[...], pltpu.roll(h, 1, axis=0))
    last = h[tm - 1:tm]
    carry[...] = last
    hl_o[0] = last
    d = prev - h

    def mix(i):
        return (h + d * mu_ref[i:i + 1]).astype(BF16)

    r = _dot(mix(0), wr_ref[...])
    zw = w0_ref[...] + _dot(jnp.tanh(_dot(mix(1), w1_ref[...])).astype(BF16), w2_ref[...])
    k = _dot(mix(2), wk_ref[...])
    v = _dot(mix(3), wv_ref[...])
    a = _sigmoid(a0_ref[...] + _dot(_dot(mix(4), a1_ref[...]).astype(BF16), a2_ref[...]))
    gg = _dot(_sigmoid(_dot(mix(5), g1_ref[...])).astype(BF16), g2_ref[...])
    lw = -math.exp(-0.5) * _sigmoid(zw)
    kk = k * kk_ref[...]
    nrm = jnp.sqrt(_seg_sum(kk * kk, ones_ref[...]))
    kn = kk / jnp.maximum(nrm, 1e-12)
    k2 = k * (1.0 + (a - 1.0) * ka_ref[...])
    r_o[0] = r.astype(r_o.dtype)
    lw_o[0] = lw
    k_o[0] = k2.astype(k_o.dtype)
    v_o[0] = v.astype(v_o.dtype)
    kn_o[0] = kn.astype(kn_o.dtype)
    ba_o[0] = (kn * a).astype(ba_o.dtype)
    gg_o[0] = gg.astype(gg_o.dtype)


def rwkv_pre(x, shift0, g, lw_, ones_bd, tm):
    b, t, d = x.shape
    tok = pl.BlockSpec((1, tm, d), lambda i, j: (i, j, 0))
    row = pl.BlockSpec((1, 1, d), lambda i, j: (i, 0, 0))
    vec = _full((1, d))
    mats = [lw_["wr"], lw_["wk"], lw_["wv"], lw_["w1"], lw_["w2"], lw_["a1"], lw_["a2"], lw_["g1"], lw_["g2"]]
    vecs = [lw_["w0"], lw_["a0"], lw_["k_k"], lw_["k_a"]]
    outs = pl.pallas_call(
        functools.partial(_rwkv_pre_kernel, tm=tm, eps=NORM_EPS),
        grid=(b, t // tm),
        in_specs=[tok, row, vec, _full((6, d))] + [_full(m.shape) for m in mats] + [vec] * 4 + [_full((256, 256))],
        out_specs=[tok] * 7 + [row],
        out_shape=[jax.ShapeDtypeStruct((b, t, d), BF16), jax.ShapeDtypeStruct((b, t, d), F32)]
        + [jax.ShapeDtypeStruct((b, t, d), BF16)] * 5 + [jax.ShapeDtypeStruct((b, 1, d), F32)],
        scratch_shapes=[pltpu.VMEM((1, d), F32)],
        compiler_params=_cparams("parallel", "arbitrary"),
        name="rwkv_pre",
    )(x, shift0.reshape(b, 1, d), g.reshape(1, d), lw_["mu"], *mats, *[u.reshape(1, d) for u in vecs], ones_bd)
    return outs


def _rwkv_scan_kernel(r_ref, lw_ref, k_ref, v_ref, kn_ref, ba_ref, s0_ref, y_ref, sout_ref, s_scr, *, L):
    c = pl.program_id(1)

    @pl.when(c == 0)
    def _():
        s_scr[...] = s0_ref[0]

    lw = lw_ref[0]
    pc = _cumsum_rows(lw)
    plast = pc[L - 1:L]
    e_n = jnp.exp(-pc)
    e_l = jnp.exp(plast - pc)
    rt = (r_ref[0].astype(F32) * jnp.exp(pc)).astype(BF16)
    at = (-kn_ref[0].astype(F32) * jnp.exp(pc - lw)).astype(BF16)
    ba = ba_ref[0].astype(F32)
    kf = k_ref[0].astype(F32)
    bt = (ba * e_n).astype(BF16)
    kt = (kf * e_n).astype(BF16)
    bl = (ba * e_l).astype(BF16)
    kl = (kf * e_l).astype(BF16)
    dec = jnp.exp(plast)
    v = v_ref[0]
    m0 = _head0_mask(HEAD_PAIR, RW_HD)
    m0s = _head0_mask(2 * L, L)
    rr = lax.broadcasted_iota(jnp.int32, (L, 2 * L), 0)
    cc = lax.broadcasted_iota(jnp.int32, (L, 2 * L), 1)
    cm = jnp.where(cc >= L, cc - L, cc)
    strict = cm < rr
    incl = cm <= rr
    eye = jnp.where(cm == rr, 1.0, 0.0)
    br = lax.broadcasted_iota(jnp.int32, (HEAD_PAIR, HEAD_PAIR), 0)
    bc = lax.broadcasted_iota(jnp.int32, (HEAD_PAIR, HEAD_PAIR), 1)
    blockdiag = (br < RW_HD) == (bc < RW_HD)
    nsq = max(1, int(math.ceil(math.log2(L))))
    for p in range(RW_HEADS // 2):
        sl = slice(HEAD_PAIR * p, HEAD_PAIR * (p + 1))
        ar = jnp.concatenate([at[:, sl], rt[:, sl]], axis=0)
        bk = jnp.concatenate([_bd(bt[:, sl], m0), _bd(kt[:, sl], m0)], axis=0)
        sc = _dot_nt(ar, bk)
        a_ab = jnp.where(strict, sc[:L, :2 * L], 0.0)
        a_ak = jnp.where(strict, sc[:L, 2 * L:], 0.0)
        a_rb = jnp.where(incl, sc[L:, :2 * L], 0.0)
        a_rk = jnp.where(incl, sc[L:, 2 * L:], 0.0)
        st = s_scr[p]
        sprod = _dot_nt(ar, st.astype(BF16))
        v_p = v[:, sl]
        vbd = _bd(v_p, m0)
        akv = _dot(jnp.concatenate([a_ak, a_rk], axis=0).astype(BF16), vbd)
        rhs = sprod[:L] + akv[:L]
        pw = a_ab
        tm_ = eye + pw
        for i in range(1, nsq):
            pwb = _bd(pw.astype(BF16), m0s)
            pw = _dot(pw.astype(BF16), pwb)
            tm_ = tm_ + _dot(tm_.astype(BF16), _bd(pw.astype(BF16), m0s))
        u = _dot(tm_.astype(BF16), _bd(rhs.astype(BF16), m0))
        u_b = u.astype(BF16)
        y = sprod[L:] + akv[L:] + _dot(a_rb.astype(BF16), _bd(u_b, m0))
        y_ref[0, :, sl] = y.astype(y_ref.dtype)
        upd = _dot_tn(jnp.concatenate([u_b, v_p], axis=0), jnp.concatenate([bl[:, sl], kl[:, sl]], axis=0))
        s_scr[p] = st * dec[:, sl] + jnp.where(blockdiag, upd, 0.0)

    @pl.when(c == pl.num_programs(1) - 1)
    def _():
        sout_ref[0] = s_scr[...]


def rwkv_scan(r, lw, k2, v, kn, ba, s0, L):
    b, t, d = r.shape
    hp = RW_HEADS // 2
    tok = pl.BlockSpec((1, L, d), lambda i, j: (i, j, 0))
    st_spec = pl.BlockSpec((1, hp, HEAD_PAIR, HEAD_PAIR), lambda i, j: (i, 0, 0, 0))
    return pl.pallas_call(
        functools.partial(_rwkv_scan_kernel, L=L),
        grid=(b, t // L),
        in_specs=[tok] * 6 + [st_spec],
        out_specs=[tok, st_spec],
        out_shape=[jax.ShapeDtypeStruct((b, t, d), F32), jax.ShapeDtypeStruct((b, hp, HEAD_PAIR, HEAD_PAIR), F32)],
        scratch_shapes=[pltpu.VMEM((hp, HEAD_PAIR, HEAD_PAIR), F32)],
        compiler_params=_cparams("parallel", "arbitrary"),
        name="rwkv_scan",
    )(r, lw, k2, v, kn, ba, s0)


def _rwkv_post_kernel(x_ref, y_ref, r_ref, k_ref, v_ref, gg_ref, lnw_ref, lnb_ref, rk_ref, ones_ref, wo_ref, o_ref):
    ones = ones_ref[...]
    y = y_ref[...]
    inv = 1.0 / RW_HD
    mean = _seg_sum(y, ones) * inv
    yc = y - mean
    var = _seg_sum(yc * yc, ones) * inv
    yn = yc * lax.rsqrt(var + RW_LN_EPS) * lnw_ref[...] + lnb_ref[...]
    v = v_ref[...].astype(F32)
    bonus = _seg_sum(r_ref[...].astype(F32) * k_ref[...].astype(F32) * rk_ref[...], ones)
    z = ((yn + bonus * v) * gg_ref[...].astype(F32)).astype(BF16)
    o_ref[...] = x_ref[...] + _dot(z, wo_ref[...])


def rwkv_post(x2, y, r, k2, v, gg, lnw, lnb, rk, ones_bd, wo, tm):
    n, d = x2.shape
    tok = pl.BlockSpec((tm, d), lambda i: (i, 0))
    vec = _full((1, d))
    return pl.pallas_call(
        _rwkv_post_kernel,
        grid=(n // tm,),
        in_specs=[tok] * 6 + [vec] * 3 + [_full((256, 256)), _full((d, d))],
        out_specs=tok,
        out_shape=jax.ShapeDtypeStruct((n, d), F32),
        compiler_params=_cparams("parallel"),
        name="rwkv_post",
    )(x2, y, r, k2, v, gg, lnw.reshape(1, d), lnb.reshape(1, d), rk.reshape(1, d), ones_bd, wo)


def _gla_state_in(s):
    b = s.shape[0]
    st = jnp.swapaxes(s.astype(F32), 2, 3).reshape(b, GLA_HEADS // 2, 2, GLA_DV, GLA_DK)
    return jnp.transpose(st, (0, 1, 3, 2, 4)).reshape(b, GLA_HEADS // 2, GLA_DV, 2 * GLA_DK)


def _gla_state_out(st):
    b = st.shape[0]
    s = st.reshape(b, GLA_HEADS // 2, GLA_DV, 2, GLA_DK)
    s = jnp.transpose(s, (0, 1, 3, 2, 4)).reshape(b, GLA_HEADS, GLA_DV, GLA_DK)
    return jnp.swapaxes(s, 2, 3)


def _rw_state_in(s):
    b = s.shape[0]
    sp = s.astype(F32).reshape(b, RW_HEADS // 2, 2, RW_HD, RW_HD)
    z = jnp.zeros_like(sp[:, :, 0])
    top = jnp.concatenate([sp[:, :, 0], z], axis=-1)
    bot = jnp.concatenate([z, sp[:, :, 1]], axis=-1)
    return jnp.concatenate([top, bot], axis=-2)


def _rw_state_out(sb):
    b = sb.shape[0]
    s0 = sb[:, :, :RW_HD, :RW_HD]
    s1 = sb[:, :, RW_HD:, RW_HD:]
    return jnp.stack([s0, s1], axis=2).reshape(b, RW_HEADS, RW_HD, RW_HD)


def _dup_heads(k):
    k0, k1 = k[..., :SWA_HD], k[..., SWA_HD:]
    return jnp.concatenate([k0, k0, k1, k1], axis=-1)


def _token_tile(n, cap):
    tm = min(n, cap)
    while n % tm:
        tm //= 2
    return tm


def kernel(x_prompt, x_sample, cache_swa_k, cache_swa_v, state_gla, state_rwkv, state_shift, cache_mem_k, cache_mem_v, mem_prompt, rel_bias, norm_mix, norm_xattn, norm_ffn, norm_mem, norm_final, ab_w_in, gla_w_a2, gla_b_a, gla_onorm, swa_sink, ab_w_out, rw_mu, rw_wr, rw_wk, rw_wv, rw_wo, rw_w0, rw_w1, rw_w2, rw_a0, rw_a1, rw_a2, rw_g1, rw_g2, rw_kk, rw_ka, rw_rk, rw_lnx_w, rw_lnx_b, xa_wq, xa_wk, xa_wv, xa_wo, ffn_w1, ffn_w3, ffn_w2):
    D = D_MODEL
    n_even = ab_w_in.shape[0]
    n_odd = rw_wr.shape[0]
    depth = xa_wq.shape[0]

    o_q, o_k, o_v, o_g, o_lr = 0, GLA_QK, 2 * GLA_QK, 2 * GLA_QK + GLA_V, 2 * GLA_QK + 2 * GLA_V
    o_sq = o_lr + GLA_LR
    o_sk = o_sq + SWA_HEADS * SWA_HD
    o_sv = o_sk + SWA_KV_HEADS * SWA_HD
    even_w = []
    for i in range(n_even):
        w = ab_w_in[i]
        wsk = w[:, o_sk:o_sv]
        wsv = w[:, o_sv:o_sv + SWA_KV_HEADS * SWA_HD]
        wlr = jnp.pad(w[:, o_lr:o_sq], ((0, 0), (0, LANES - GLA_LR)))
        w_all = jnp.concatenate([
            w[:, o_q:o_k] * (GLA_DK ** -0.5), w[:, o_k:o_v], w[:, o_v:o_g], w[:, o_g:o_lr],
            w[:, o_sq:o_sk] * (SWA_HD ** -0.5), wsk, wsv, _dup_heads(wsk), _dup_heads(wsv), wlr], axis=1).astype(BF16)
        even_w.append(dict(
            w_all=w_all,
            wa2p=jnp.pad(gla_w_a2[i], ((0, LANES - GLA_LR), (0, 0))).astype(BF16),
            wo_a=ab_w_out[i][:GLA_V].astype(BF16), wo_b=ab_w_out[i][GLA_V:].astype(BF16)))
    even_splits = (GLA_QK, GLA_QK, GLA_V, GLA_V, SWA_HEADS * SWA_HD, 2 * SWA_HD, 2 * SWA_HD, 4 * SWA_HD, 4 * SWA_HD, LANES)
    even_dtypes = (BF16, BF16, BF16, BF16, BF16, F32, F32, BF16, BF16, F32)
    odd_w = []
    for i in range(n_odd):
        odd_w.append(dict(
            mu=rw_mu[i], wr=rw_wr[i].astype(BF16), wk=rw_wk[i].astype(BF16), wv=rw_wv[i].astype(BF16),
            w1=rw_w1[i].astype(BF16), w2=rw_w2[i].astype(BF16), a1=rw_a1[i].astype(BF16), a2=rw_a2[i].astype(BF16),
            g1=rw_g1[i].astype(BF16), g2=rw_g2[i].astype(BF16), w0=rw_w0[i], a0=rw_a0[i], k_k=rw_kk[i], k_a=rw_ka[i],
            wo=rw_wo[i].astype(BF16)))
    wq_b = [(xa_wq[l] * (X_HD ** -0.5)).astype(BF16) for l in range(depth)]
    wo_b = [xa_wo[l].astype(BF16) for l in range(depth)]
    wkv_b = [jnp.concatenate([xa_wk[l], xa_wv[l]], axis=1).astype(BF16) for l in range(depth)]
    f1 = [ffn_w1[l].astype(BF16) for l in range(depth)]
    f3 = [ffn_w3[l].astype(BF16) for l in range(depth)]
    f2 = [ffn_w2[l].astype(BF16) for l in range(depth)]
    gi = np.arange(256) // RW_HD
    ones_bd = jnp.asarray((gi[:, None] == gi[None, :]).astype(np.float32)).astype(BF16)

    def run(x, mem_k, mem_v, swa_k, swa_v, gla_s, rw_s, rw_shift):
        prompt = gla_s is None
        B, T, _ = x.shape
        n = B * T
        L = min(CHUNK, T)
        tq = min(WINDOW, T)
        tm = _token_tile(n, 512)
        tmb = _token_tile(T, 512)
        tm_ffn = _token_tile(n, 256)
        bias = swa_bias(rel_bias, tq, WINDOW + tq, chunked=prompt)
        out_k, out_v, out_gla, out_rw, out_shift = [], [], [], [], []
        x2 = x.reshape(n, D)
        y2 = None
        for l in range(depth):
            i = l // 2
            if l % 2 == 0:
                ew = even_w[i]
                q, k, v, gate, sq, sk, sv, kdup, vdup, lr = norm_proj(x2, norm_mix[l], ew["w_all"], even_splits, even_dtypes, tm)
                s0t = jnp.zeros((B, GLA_HEADS // 2, GLA_DV, 2 * GLA_DK), F32) if prompt else _gla_state_in(gla_s[i])
                o_a, st = gla_mixer(q.reshape(B, T, -1), k.reshape(B, T, -1), v.reshape(B, T, -1), gate.reshape(B, T, -1),
                                    lr.reshape(B, T, -1), ew["wa2p"], gla_b_a[i], gla_onorm[i], s0t, L)
                sk3 = sk.reshape(B, T, 2 * SWA_HD)
                sv3 = sv.reshape(B, T, 2 * SWA_HD)
                if prompt:
                    pre_k = jnp.zeros((B, WINDOW, 4 * SWA_HD), BF16)
                    pre_v = pre_k
                    new_k, new_v = sk3[:, -WINDOW:], sv3[:, -WINDOW:]
                else:
                    ck = swa_k[i].reshape(B, WINDOW, 2 * SWA_HD)
                    cv = swa_v[i].reshape(B, WINDOW, 2 * SWA_HD)
                    pre_k = _dup_heads(ck).astype(BF16)
                    pre_v = _dup_heads(cv).astype(BF16)
                    new_k = jnp.concatenate([ck, sk3], axis=1)[:, -WINDOW:]
                    new_v = jnp.concatenate([cv, sv3], axis=1)[:, -WINDOW:]
                k_ext = jnp.concatenate([pre_k, kdup.reshape(B, T, -1)], axis=1)
                v_ext = jnp.concatenate([pre_v, vdup.reshape(B, T, -1)], axis=1)
                o_b = swa_mixer(sq.reshape(B, T, -1), k_ext, v_ext, bias, swa_sink[i], tq, prefix_valid=not prompt)
                x2 = outproj_residual(x2, o_a.reshape(n, -1), o_b.reshape(n, -1), ew["wo_a"], ew["wo_b"], tm)
                out_k.append(new_k.reshape(B, WINDOW, SWA_KV_HEADS, SWA_HD))
                out_v.append(new_v.reshape(B, WINDOW, SWA_KV_HEADS, SWA_HD))
                out_gla.append(_gla_state_out(st))
            else:
                ow = odd_w[i]
                shift0 = jnp.zeros((B, D), F32) if prompt else rw_shift[i]
                s0 = jnp.zeros((B, RW_HEADS // 2, HEAD_PAIR, HEAD_PAIR), F32) if prompt else _rw_state_in(rw_s[i])
                r, lw, k2, v, kn, ba, gg, hl = rwkv_pre(x2.reshape(B, T, D), shift0, norm_mix[l], ow, ones_bd, tmb)
                y, sb = rwkv_scan(r, lw, k2, v, kn, ba, s0, L)
                x2 = rwkv_post(x2, y.reshape(n, D), r.reshape(n, D), k2.reshape(n, D), v.reshape(n, D), gg.reshape(n, D),
                               rw_lnx_w[i], rw_lnx_b[i], rw_rk[i].reshape(D), ones_bd, ow["wo"], tm)
                out_rw.append(_rw_state_out(sb))
                out_shift.append(hl.reshape(B, D))
            mk = mem_k[l].reshape(B, -1, D).astype(BF16)
            mv = mem_v[l].reshape(B, -1, D).astype(BF16)
            x2 = xattn_residual(x2.reshape(B, T, D), norm_xattn[l], wq_b[l], mk, mv, wo_b[l], tmb).reshape(n, D)
            res = ffn_residual(x2, norm_ffn[l], f1[l], f3[l], f2[l], norm_final, tm_ffn, with_final=(l == depth - 1))
            x2 = res[0]
            y2 = res[-1]
        return (y2.reshape(B, T, D), jnp.stack(out_k), jnp.stack(out_v), jnp.stack(out_gla), jnp.stack(out_rw),
                jnp.stack(out_shift))

    Bp, M, _ = mem_prompt.shape
    mem2 = mem_prompt.reshape(Bp * M, D)
    pk, pv = [], []
    for l in range(depth):
        kk_, vv_ = norm_proj(mem2, norm_mem[l], wkv_b[l], (D, D), (F32, F32), _token_tile(Bp * M, 512))
        pk.append(kk_.reshape(Bp, M, X_HEADS, X_HD))
        pv.append(vv_.reshape(Bp, M, X_HEADS, X_HD))
    p_mem_k = jnp.stack(pk)
    p_mem_v = jnp.stack(pv)
    y_prompt, p_swa_k, p_swa_v, p_gla, p_rwkv, p_shift = run(x_prompt, p_mem_k, p_mem_v, None, None, None, None, None)
    y_sample, s_swa_k, s_swa_v, s_gla, s_rwkv, s_shift = run(
        x_sample, cache_mem_k, cache_mem_v, cache_swa_k, cache_swa_v, state_gla, state_rwkv, state_shift)
    return (y_prompt, y_sample, p_swa_k, p_swa_v, p_gla, p_rwkv, p_shift, p_mem_k, p_mem_v,
            s_swa_k, s_swa_v, s_gla, s_rwkv, s_shift)
```

```python
import functools
import math

import jax
import jax.numpy as jnp
import numpy as np
from jax import lax
from jax.experimental import pallas as pl
from jax.experimental.pallas import tpu as pltpu

F32 = jnp.float32
BF16 = jnp.bfloat16

D_MODEL = 1024
DEPTH = 4
CHUNK = 64
NORM_EPS = 1e-6
NEG_INF = -1e30
GLA_HEADS = 4
GLA_DK = 64
GLA_DV = 128
GLA_QK = GLA_HEADS * GLA_DK
GLA_V = GLA_HEADS * GLA_DV
GLA_LR = 16
GLA_TAU = 16.0
SWA_HEADS = 8
SWA_KV_HEADS = 2
SWA_GROUP = SWA_HEADS // SWA_KV_HEADS
SWA_HD = 64
WINDOW = 128
N_BUCKETS = 32
MAX_DIST = 128
RW_HD = 64
RW_HEADS = D_MODEL // RW_HD
RW_LN_EPS = 64e-5
X_HEADS = 4
X_HD = D_MODEL // X_HEADS

LANES = 128
HEAD_PAIR = 2 * RW_HD
VMEM_LIMIT = 56 * 1024 * 1024


def _cparams(*sem):
    return pltpu.CompilerParams(dimension_semantics=sem, vmem_limit_bytes=VMEM_LIMIT)


def _dot(a, b):
    return jnp.dot(a, b, preferred_element_type=F32)


def _dot_nt(a, b):
    return lax.dot_general(a, b, (((1,), (1,)), ((), ())), preferred_element_type=F32)


def _dot_tn(a, b):
    return lax.dot_general(a, b, (((0,), (0,)), ((), ())), preferred_element_type=F32)


def _rms(x, g, eps):
    return x * lax.rsqrt(jnp.mean(x * x, axis=-1, keepdims=True) + eps) * g


def _sigmoid(x):
    return 1.0 / (1.0 + jnp.exp(-x))


def _log_sigmoid(x):
    return jnp.minimum(x, 0.0) - jnp.log(1.0 + jnp.exp(-jnp.abs(x)))


def _cumsum_rows(x):
    n = x.shape[0]
    r = lax.broadcasted_iota(jnp.int32, (n, n), 0)
    c = lax.broadcasted_iota(jnp.int32, (n, n), 1)
    tri = jnp.where(c <= r, 1.0, 0.0).astype(BF16)
    hi = x.astype(BF16)
    r1 = x - hi.astype(F32)
    mid = r1.astype(BF16)
    lo = (r1 - mid.astype(F32)).astype(BF16)
    return _dot(tri, hi) + _dot(tri, mid) + _dot(tri, lo)


def _head0_mask(width, half):
    return lax.broadcasted_iota(jnp.int32, (1, width), 1) < half


def _bd(x, m0):
    zero = jnp.zeros_like(x)
    return jnp.concatenate([jnp.where(m0, x, zero), jnp.where(m0, zero, x)], axis=0)


def _seg_sum(x, ones_bd):
    parts = []
    for j in range(x.shape[1] // 256):
        parts.append(_dot(x[:, 256 * j:256 * (j + 1)].astype(BF16), ones_bd))
    return jnp.concatenate(parts, axis=1)


def _full(shape):
    nd = len(shape)
    return pl.BlockSpec(shape, lambda *_: (0,) * nd, pipeline_mode=pl.Buffered(1))


def _norm_proj_kernel(x_ref, g_ref, w_ref, *out_refs, splits, eps):
    h = _rms(x_ref[...], g_ref[...], eps).astype(BF16)
    p = _dot(h, w_ref[...])
    off = 0
    for o_ref, width in zip(out_refs, splits):
        o_ref[...] = p[:, off:off + width].astype(o_ref.dtype)
        off += width


def norm_proj(x2, g, w, splits, dtypes, tm):
    n, d = x2.shape
    ntot = w.shape[1]
    assert sum(splits) == ntot and n % tm == 0
    return pl.pallas_call(
        functools.partial(_norm_proj_kernel, splits=tuple(splits), eps=NORM_EPS),
        grid=(n // tm,),
        in_specs=[pl.BlockSpec((tm, d), lambda i: (i, 0)), _full((1, d)), _full((d, ntot))],
        out_specs=[pl.BlockSpec((tm, s), lambda i: (i, 0)) for s in splits],
        out_shape=[jax.ShapeDtypeStruct((n, s), dt) for s, dt in zip(splits, dtypes)],
        compiler_params=_cparams("parallel"),
        name="norm_proj",
    )(x2, g.reshape(1, d), w)


def _outproj_kernel(x_ref, a_ref, b_ref, wa_ref, wb_ref, o_ref):
    o_ref[...] = x_ref[...] + _dot(a_ref[...], wa_ref[...]) + _dot(b_ref[...], wb_ref[...])


def outproj_residual(x2, a, b, wa, wb, tm):
    n, d = x2.shape
    ka, kb = a.shape[1], b.shape[1]
    return pl.pallas_call(
        _outproj_kernel,
        grid=(n // tm,),
        in_specs=[pl.BlockSpec((tm, d), lambda i: (i, 0)), pl.BlockSpec((tm, ka), lambda i: (i, 0)),
                  pl.BlockSpec((tm, kb), lambda i: (i, 0)), _full((ka, d)), _full((kb, d))],
        out_specs=pl.BlockSpec((tm, d), lambda i: (i, 0)),
        out_shape=jax.ShapeDtypeStruct((n, d), F32),
        compiler_params=_cparams("parallel"),
        name="outproj_residual",
    )(x2, a, b, wa, wb)


def _xattn_kernel(x_ref, g_ref, wq_ref, mk_ref, mv_ref, wo_ref, o_ref, *, eps):
    x = x_ref[0]
    h = _rms(x, g_ref[...], eps).astype(BF16)
    q = _dot(h, wq_ref[...]).astype(BF16)
    outs = []
    for hh in range(X_HEADS):
        sl = slice(X_HD * hh, X_HD * (hh + 1))
        s = _dot_nt(q[:, sl], mk_ref[0, :, sl])
        m = jnp.max(s, axis=-1, keepdims=True)
        ex = jnp.exp(s - m)
        den = jnp.sum(ex, axis=-1, keepdims=True)
        outs.append((_dot(ex.astype(BF16), mv_ref[0, :, sl]) / den).astype(BF16))
    o = jnp.concatenate(outs, axis=1)
    o_ref[0] = x + _dot(o, wo_ref[...])


def xattn_residual(x, g, wq, mk, mv, wo, tm):
    b, t, d = x.shape
    m = mk.shape[1]
    return pl.pallas_call(
        functools.partial(_xattn_kernel, eps=NORM_EPS),
        grid=(b, t // tm),
        in_specs=[pl.BlockSpec((1, tm, d), lambda i, j: (i, j, 0)), _full((1, d)), _full((d, d)),
                  pl.BlockSpec((1, m, d), lambda i, j: (i, 0, 0)), pl.BlockSpec((1, m, d), lambda i, j: (i, 0, 0)),
                  _full((d, d))],
        out_specs=pl.BlockSpec((1, tm, d), lambda i, j: (i, j, 0)),
        out_shape=jax.ShapeDtypeStruct((b, t, d), F32),
        compiler_params=_cparams("parallel", "parallel"),
        name="xattn_residual",
    )(x, g.reshape(1, d), wq, mk, mv, wo)


def _ffn_kernel(x_ref, g_ref, w1_ref, w3_ref, w2_ref, gf_ref, o_ref, *y_ref, eps):
    x = x_ref[...]
    h = _rms(x, g_ref[...], eps).astype(BF16)
    a = _dot(h, w1_ref[...])
    b = _dot(h, w3_ref[...])
    u = (a * _sigmoid(a) * b).astype(BF16)
    o = x + _dot(u, w2_ref[...])
    o_ref[...] = o
    if y_ref:
        y_ref[0][...] = _rms(o, gf_ref[...], eps)


def ffn_residual(x2, g, w1, w3, w2, gf, tm, with_final):
    n, d = x2.shape
    f = w1.shape[1]
    tok = pl.BlockSpec((tm, d), lambda i: (i, 0))
    n_out = 2 if with_final else 1
    return pl.pallas_call(
        functools.partial(_ffn_kernel, eps=NORM_EPS),
        grid=(n // tm,),
        in_specs=[tok, _full((1, d)), _full((d, f)), _full((d, f)), _full((f, d)), _full((1, d))],
        out_specs=[tok] * n_out,
        out_shape=[jax.ShapeDtypeStruct((n, d), F32)] * n_out,
        compiler_params=_cparams("parallel"),
        name="ffn_residual",
    )(x2, g.reshape(1, d), w1, w3, w2, gf.reshape(1, d))


def _gla_kernel(q_ref, k_ref, v_ref, gate_ref, lr_ref, wa2_ref, ba_ref, on_ref, s0_ref, o_ref, sout_ref, s_scr, *, L, eps):
    c = pl.program_id(1)

    @pl.when(c == 0)
    def _():
        s_scr[...] = s0_ref[0]

    z = _dot(lr_ref[0].astype(BF16), wa2_ref[...]) + ba_ref[...]
    b = _cumsum_rows(_log_sigmoid(z) * (1.0 / GLA_TAU))
    mrow = b[L // 2:L // 2 + 1]
    blast = b[L - 1:L]
    qd = q_ref[0].astype(F32) * jnp.exp(b - mrow)
    kd = k_ref[0].astype(F32) * jnp.exp(mrow - b)
    qb = qd * jnp.exp(mrow)
    kl = kd * jnp.exp(blast - mrow)
    dec = jnp.exp(blast)
    m0 = _head0_mask(HEAD_PAIR, GLA_DK)
    m0s = _head0_mask(2 * L, L)
    rr = lax.broadcasted_iota(jnp.int32, (L, 2 * L), 0)
    cc = lax.broadcasted_iota(jnp.int32, (L, 2 * L), 1)
    causal = jnp.where(cc >= L, cc - L, cc) <= rr
    v = v_ref[0]
    gate = gate_ref[0].astype(F32)
    for p in range(GLA_HEADS // 2):
        sl = slice(HEAD_PAIR * p, HEAD_PAIR * (p + 1))
        qd_p = qd[:, sl].astype(BF16)
        kd_p = kd[:, sl].astype(BF16)
        qb_p = qb[:, sl].astype(BF16)
        kl_p = kl[:, sl].astype(BF16)
        att = _dot_nt(qd_p, _bd(kd_p, m0))
        att = jnp.where(causal, att, 0.0).astype(BF16)
        st = s_scr[p]
        st_b = st.astype(BF16)
        vst = jnp.concatenate([v[:, GLA_DV * (2 * p):GLA_DV * (2 * p + 1)],
                               v[:, GLA_DV * (2 * p + 1):GLA_DV * (2 * p + 2)]], axis=0)
        snew = st * dec[:, sl]
        for e in range(2):
            h = 2 * p + e
            hs = slice(GLA_DV * h, GLA_DV * (h + 1))
            me = m0 if e == 0 else jnp.logical_not(m0)
            mes = m0s if e == 0 else jnp.logical_not(m0s)
            o = _dot(jnp.where(mes, att, jnp.zeros_like(att)), vst)
            o = o + _dot_nt(jnp.where(me, qb_p, jnp.zeros_like(qb_p)), st_b)
            snew = snew + _dot_tn(v[:, hs], jnp.where(me, kl_p, jnp.zeros_like(kl_p)))
            o = _rms(o, on_ref[...], eps)
            g = gate[:, hs]
            o_ref[0, :, hs] = (o * (g * _sigmoid(g))).astype(o_ref.dtype)
        s_scr[p] = snew

    @pl.when(c == pl.num_programs(1) - 1)
    def _():
        sout_ref[0] = s_scr[...]


def gla_mixer(q, k, v, gate, lr, wa2p, ba, onorm, s0t, L):
    b, t, _ = q.shape
    hp = GLA_HEADS // 2

    def tok(width):
        return pl.BlockSpec((1, L, width), lambda i, j: (i, j, 0))

    st_spec = pl.BlockSpec((1, hp, GLA_DV, HEAD_PAIR), lambda i, j: (i, 0, 0, 0))
    return pl.pallas_call(
        functools.partial(_gla_kernel, L=L, eps=NORM_EPS),
        grid=(b, t // L),
        in_specs=[tok(GLA_QK), tok(GLA_QK), tok(GLA_V), tok(GLA_V), tok(LANES), _full((LANES, GLA_QK)),
                  _full((1, GLA_QK)), _full((1, GLA_DV)), st_spec],
        out_specs=[tok(GLA_V), st_spec],
        out_shape=[jax.ShapeDtypeStruct((b, t, GLA_V), BF16), jax.ShapeDtypeStruct((b, hp, GLA_DV, HEAD_PAIR), F32)],
        scratch_shapes=[pltpu.VMEM((hp, GLA_DV, HEAD_PAIR), F32)],
        compiler_params=_cparams("parallel", "arbitrary"),
        name="gla_mixer",
    )(q, k, v, gate, lr, wa2p, ba.reshape(1, GLA_QK), onorm.reshape(1, GLA_DV), s0t)


def _bias_kernel(tab_ref, bucket_ref, valid_ref, o_ref):
    bk = bucket_ref[...]
    ok = valid_ref[...] != 0
    for h in range(SWA_HEADS):
        acc = jnp.zeros(bk.shape, F32)
        for n in range(N_BUCKETS):
            acc = jnp.where(bk == n, tab_ref[n, h], acc)
        o_ref[h] = jnp.where(ok, acc, NEG_INF)


def swa_bias(table, tq, lk, chunked):
    rel = jnp.arange(lk)[None, :] - WINDOW - jnp.arange(tq)[:, None]
    nb = N_BUCKETS // 2
    max_exact = nb // 2
    n = jnp.abs(rel)
    nf = jnp.maximum(n, 1).astype(F32)
    large = max_exact + (jnp.log(nf / max_exact) / math.log(MAX_DIST / max_exact) * (nb - max_exact)).astype(jnp.int32)
    large = jnp.minimum(large, nb - 1)
    bucket = (jnp.where(rel > 0, nb, 0) + jnp.where(n < max_exact, n, large)).astype(jnp.int32)
    if chunked:
        qc = np.arange(tq)[:, None] // CHUNK
        kc = np.arange(lk)[None, :] // CHUNK
        valid = ((kc >= qc) & (kc <= qc + WINDOW // CHUNK)).astype(np.int32)
    else:
        valid = np.ones((tq, lk), np.int32)
    return pl.pallas_call(
        _bias_kernel,
        in_specs=[pl.BlockSpec(memory_space=pltpu.SMEM), _full((tq, lk)), _full((tq, lk))],
        out_specs=_full((SWA_HEADS, tq, lk)),
        out_shape=jax.ShapeDtypeStruct((SWA_HEADS, tq, lk), F32),
        grid=(1,),
        name="swa_bias",
    )(table.astype(F32), bucket, jnp.asarray(valid))


def _swa_kernel(sink_ref, q_ref, k_ref, v_ref, bias_ref, o_ref, *, tq, lk, prefix_valid):
    c = pl.program_id(1)
    start = pl.multiple_of(c * tq, tq)
    kk = k_ref[0, pl.ds(start, lk), :]
    vv = v_ref[0, pl.ds(start, lk), :]
    q = q_ref[0]
    m0 = _head0_mask(HEAD_PAIR, SWA_HD)
    if not prefix_valid:
        key_ok = (lax.broadcasted_iota(jnp.int32, (tq, lk), 1) + start) >= WINDOW
    for p in range(SWA_HEADS // 2):
        j = (2 * p) // SWA_GROUP
        qp = q[:, HEAD_PAIR * p:HEAD_PAIR * (p + 1)]
        kj = kk[:, HEAD_PAIR * j:HEAD_PAIR * (j + 1)]
        vj = vv[:, HEAD_PAIR * j:HEAD_PAIR * (j + 1)]
        outs = []
        for e in range(2):
            h = 2 * p + e
            me = m0 if e == 0 else jnp.logical_not(m0)
            s = _dot_nt(jnp.where(me, qp, jnp.zeros_like(qp)), kj) + bias_ref[h]
            if not prefix_valid:
                s = jnp.where(key_ok, s, NEG_INF)
            snk = sink_ref[h]
            m = jnp.maximum(jnp.max(s, axis=-1, keepdims=True), snk)
            ex = jnp.exp(s - m)
            den = jnp.sum(ex, axis=-1, keepdims=True) + jnp.exp(snk - m)
            outs.append(_dot(ex.astype(BF16), vj) / den)
        o_ref[0, :, HEAD_PAIR * p:HEAD_PAIR * (p + 1)] = jnp.where(m0, outs[0], outs[1]).astype(o_ref.dtype)


def swa_mixer(sq, kdup_ext, vdup_ext, bias, sink, tq, prefix_valid):
    b, t, c = sq.shape
    text = kdup_ext.shape[1]
    lk = WINDOW + tq
    return pl.pallas_call(
        functools.partial(_swa_kernel, tq=tq, lk=lk, prefix_valid=prefix_valid),
        grid=(b, t // tq),
        in_specs=[pl.BlockSpec(memory_space=pltpu.SMEM),
                  pl.BlockSpec((1, tq, c), lambda i, j: (i, j, 0)),
                  pl.BlockSpec((1, text, 2 * HEAD_PAIR), lambda i, j: (i, 0, 0)),
                  pl.BlockSpec((1, text, 2 * HEAD_PAIR), lambda i, j: (i, 0, 0)),
                  _full((SWA_HEADS, tq, lk))],
        out_specs=pl.BlockSpec((1, tq, c), lambda i, j: (i, j, 0)),
        out_shape=jax.ShapeDtypeStruct((b, t, c), BF16),
        compiler_params=_cparams("parallel", "parallel"),
        name="swa_mixer",
    )(sink.astype(F32), sq, kdup_ext, vdup_ext, bias)


def _rwkv_pre_kernel(x_ref, sh_ref, g_ref, mu_ref, wr_ref, wk_ref, wv_ref, w1_ref, w2_ref, a1_ref, a2_ref, g1_ref,
                     g2_ref, w0_ref, a0_ref, kk_ref, ka_ref, ones_ref,
                     r_o, lw_o, k_o, v_o, kn_o, ba_o, gg_o, hl_o, carry, *, tm, eps):
    t = pl.program_id(1)

    @pl.when(t == 0)
    def _():
        carry[...] = sh_ref[0]

    h = _rms(x_ref[0], g_ref[...], eps)
    row = lax.broadcasted_iota(jnp.int32, h.shape, 0)
    prev = jnp.where(row == 0, carry[...], pltpu.roll(h, 1, axis=0))
    last = h[tm - 1:tm]
    carry[...] = last
    hl_o[0] = last
    d = prev - h

    def mix(i):
        return (h + d * mu_ref[i:i + 1]).astype(BF16)

    r = _dot(mix(0), wr_ref[...])
    zw = w0_ref[...] + _dot(jnp.tanh(_dot(mix(1), w1_ref[...])).astype(BF16), w2_ref[...])
    k = _dot(mix(2), wk_ref[...])
    v = _dot(mix(3), wv_ref[...])
    a = _sigmoid(a0_ref[...] + _dot(_dot(mix(4), a1_ref[...]).astype(BF16), a2_ref[...]))
    gg = _dot(_sigmoid(_dot(mix(5), g1_ref[...])).astype(BF16), g2_ref[...])
    lw = -math.exp(-0.5) * _sigmoid(zw)
    kk = k * kk_ref[...]
    nrm = jnp.sqrt(_seg_sum(kk * kk, ones_ref[...]))
    kn = kk / jnp.maximum(nrm, 1e-12)
    k2 = k * (1.0 + (a - 1.0) * ka_ref[...])
    r_o[0] = r.astype(r_o.dtype)
    lw_o[0] = lw
    k_o[0] = k2.astype(k_o.dtype)
    v_o[0] = v.astype(v_o.dtype)
    kn_o[0] = kn.astype(kn_o.dtype)
    ba_o[0] = (kn * a).astype(ba_o.dtype)
    gg_o[0] = gg.astype(gg_o.dtype)


def rwkv_pre(x, shift0, g, lw_, ones_bd, tm):
    b, t, d = x.shape
    tok = pl.BlockSpec((1, tm, d), lambda i, j: (i, j, 0))
    row = pl.BlockSpec((1, 1, d), lambda i, j: (i, 0, 0))
    vec = _full((1, d))
    mats = [lw_["wr"], lw_["wk"], lw_["wv"], lw_["w1"], lw_["w2"], lw_["a1"], lw_["a2"], lw_["g1"], lw_["g2"]]
    vecs = [lw_["w0"], lw_["a0"], lw_["k_k"], lw_["k_a"]]
    outs = pl.pallas_call(
        functools.partial(_rwkv_pre_kernel, tm=tm, eps=NORM_EPS),
        grid=(b, t // tm),
        in_specs=[tok, row, vec, _full((6, d))] + [_full(m.shape) for m in mats] + [vec] * 4 + [_full((256, 256))],
        out_specs=[tok] * 7 + [row],
        out_shape=[jax.ShapeDtypeStruct((b, t, d), BF16), jax.ShapeDtypeStruct((b, t, d), F32)]
        + [jax.ShapeDtypeStruct((b, t, d), BF16)] * 5 + [jax.ShapeDtypeStruct((b, 1, d), F32)],
        scratch_shapes=[pltpu.VMEM((1, d), F32)],
        compiler_params=_cparams("parallel", "arbitrary"),
        name="rwkv_pre",
    )(x, shift0.reshape(b, 1, d), g.reshape(1, d), lw_["mu"], *mats, *[u.reshape(1, d) for u in vecs], ones_bd)
    return outs


def _rwkv_scan_kernel(r_ref, lw_ref, k_ref, v_ref, kn_ref, ba_ref, s0_ref, y_ref, sout_ref, s_scr, *, L):
    c = pl.program_id(1)

    @pl.when(c == 0)
    def _():
        s_scr[...] = s0_ref[0]

    lw = lw_ref[0]
    pc = _cumsum_rows(lw)
    plast = pc[L - 1:L]
    e_n = jnp.exp(-pc)
    e_l = jnp.exp(plast - pc)
    rt = (r_ref[0].astype(F32) * jnp.exp(pc)).astype(BF16)
    at = (-kn_ref[0].astype(F32) * jnp.exp(pc - lw)).astype(BF16)
    ba = ba_ref[0].astype(F32)
    kf = k_ref[0].astype(F32)
    bt = (ba * e_n).astype(BF16)
    kt = (kf * e_n).astype(BF16)
    bl = (ba * e_l).astype(BF16)
    kl = (kf * e_l).astype(BF16)
    dec = jnp.exp(plast)
    v = v_ref[0]
    m0 = _head0_mask(HEAD_PAIR, RW_HD)
    m0s = _head0_mask(2 * L, L)
    rr = lax.broadcasted_iota(jnp.int32, (L, 2 * L), 0)
    cc = lax.broadcasted_iota(jnp.int32, (L, 2 * L), 1)
    cm = jnp.where(cc >= L, cc - L, cc)
    strict = cm < rr
    incl = cm <= rr
    eye = jnp.where(cm == rr, 1.0, 0.0)
    br = lax.broadcasted_iota(jnp.int32, (HEAD_PAIR, HEAD_PAIR), 0)
    bc = lax.broadcasted_iota(jnp.int32, (HEAD_PAIR, HEAD_PAIR), 1)
    blockdiag = (br < RW_HD) == (bc < RW_HD)
    nsq = max(1, int(math.ceil(math.log2(L))))
    pairs = range(RW_HEADS // 2)
    sls = [slice(HEAD_PAIR * p, HEAD_PAIR * (p + 1)) for p in pairs]
    ars = [jnp.concatenate([at[:, sl], rt[:, sl]], axis=0) for sl in sls]
    scs = [_dot_nt(ars[p], jnp.concatenate([_bd(bt[:, sls[p]], m0), _bd(kt[:, sls[p]], m0)], axis=0))
           for p in pairs]
    sts = [s_scr[p] for p in pairs]
    sprods = [_dot_nt(ars[p], sts[p].astype(BF16)) for p in pairs]
    a_ab = [jnp.where(strict, sc[:L, :2 * L], 0.0) for sc in scs]
    a_rb = [jnp.where(incl, sc[L:, :2 * L], 0.0).astype(BF16) for sc in scs]
    a_k = [jnp.concatenate([jnp.where(strict, sc[:L, 2 * L:], 0.0), jnp.where(incl, sc[L:, 2 * L:], 0.0)],
                           axis=0).astype(BF16) for sc in scs]
    akvs = [_dot(a_k[p], _bd(v[:, sls[p]], m0)) for p in pairs]
    pws = a_ab
    tms = [eye + pw for pw in pws]
    for i in range(1, nsq):
        pws = [_dot(pw.astype(BF16), _bd(pw.astype(BF16), m0s)) for pw in pws]
        tms = [t_ + _dot(t_.astype(BF16), _bd(pw.astype(BF16), m0s)) for t_, pw in zip(tms, pws)]
    us = [_dot(tms[p].astype(BF16), _bd((sprods[p][:L] + akvs[p][:L]).astype(BF16), m0)).astype(BF16)
          for p in pairs]
    for p in pairs:
        y = sprods[p][L:] + akvs[p][L:] + _dot(a_rb[p], _bd(us[p], m0))
        y_ref[0, :, sls[p]] = y.astype(y_ref.dtype)
    for p in pairs:
        sl = sls[p]
        upd = _dot_tn(jnp.concatenate([us[p], v[:, sl]], axis=0), jnp.concatenate([bl[:, sl], kl[:, sl]], axis=0))
        s_scr[p] = sts[p] * dec[:, sl] + jnp.where(blockdiag, upd, 0.0)

    @pl.when(c == pl.num_programs(1) - 1)
    def _():
        sout_ref[0] = s_scr[...]


def rwkv_scan(r, lw, k2, v, kn, ba, s0, L):
    b, t, d = r.shape
    hp = RW_HEADS // 2
    tok = pl.BlockSpec((1, L, d), lambda i, j: (i, j, 0))
    st_spec = pl.BlockSpec((1, hp, HEAD_PAIR, HEAD_PAIR), lambda i, j: (i, 0, 0, 0))
    return pl.pallas_call(
        functools.partial(_rwkv_scan_kernel, L=L),
        grid=(b, t // L),
        in_specs=[tok] * 6 + [st_spec],
        out_specs=[tok, st_spec],
        out_shape=[jax.ShapeDtypeStruct((b, t, d), F32), jax.ShapeDtypeStruct((b, hp, HEAD_PAIR, HEAD_PAIR), F32)],
        scratch_shapes=[pltpu.VMEM((hp, HEAD_PAIR, HEAD_PAIR), F32)],
        compiler_params=_cparams("parallel", "arbitrary"),
        name="rwkv_scan",
    )(r, lw, k2, v, kn, ba, s0)


def _rwkv_post_kernel(x_ref, y_ref, r_ref, k_ref, v_ref, gg_ref, lnw_ref, lnb_ref, rk_ref, ones_ref, wo_ref, o_ref):
    ones = ones_ref[...]
    y = y_ref[...]
    inv = 1.0 / RW_HD
    mean = _seg_sum(y, ones) * inv
    yc = y - mean
    var = _seg_sum(yc * yc, ones) * inv
    yn = yc * lax.rsqrt(var + RW_LN_EPS) * lnw_ref[...] + lnb_ref[...]
    v = v_ref[...].astype(F32)
    bonus = _seg_sum(r_ref[...].astype(F32) * k_ref[...].astype(F32) * rk_ref[...], ones)
    z = ((yn + bonus * v) * gg_ref[...].astype(F32)).astype(BF16)
    o_ref[...] = x_ref[...] + _dot(z, wo_ref[...])


def rwkv_post(x2, y, r, k2, v, gg, lnw, lnb, rk, ones_bd, wo, tm):
    n, d = x2.shape
    tok = pl.BlockSpec((tm, d), lambda i: (i, 0))
    vec = _full((1, d))
    return pl.pallas_call(
        _rwkv_post_kernel,
        grid=(n // tm,),
        in_specs=[tok] * 6 + [vec] * 3 + [_full((256, 256)), _full((d, d))],
        out_specs=tok,
        out_shape=jax.ShapeDtypeStruct((n, d), F32),
        compiler_params=_cparams("parallel"),
        name="rwkv_post",
    )(x2, y, r, k2, v, gg, lnw.reshape(1, d), lnb.reshape(1, d), rk.reshape(1, d), ones_bd, wo)


def _gla_state_in(s):
    b = s.shape[0]
    st = jnp.swapaxes(s.astype(F32), 2, 3).reshape(b, GLA_HEADS // 2, 2, GLA_DV, GLA_DK)
    return jnp.transpose(st, (0, 1, 3, 2, 4)).reshape(b, GLA_HEADS // 2, GLA_DV, 2 * GLA_DK)


def _gla_state_out(st):
    b = st.shape[0]
    s = st.reshape(b, GLA_HEADS // 2, GLA_DV, 2, GLA_DK)
    s = jnp.transpose(s, (0, 1, 3, 2, 4)).reshape(b, GLA_HEADS, GLA_DV, GLA_DK)
    return jnp.swapaxes(s, 2, 3)


def _rw_state_in(s):
    b = s.shape[0]
    sp = s.astype(F32).reshape(b, RW_HEADS // 2, 2, RW_HD, RW_HD)
    z = jnp.zeros_like(sp[:, :, 0])
    top = jnp.concatenate([sp[:, :, 0], z], axis=-1)
    bot = jnp.concatenate([z, sp[:, :, 1]], axis=-1)
    return jnp.concatenate([top, bot], axis=-2)


def _rw_state_out(sb):
    b = sb.shape[0]
    s0 = sb[:, :, :RW_HD, :RW_HD]
    s1 = sb[:, :, RW_HD:, RW_HD:]
    return jnp.stack([s0, s1], axis=2).reshape(b, RW_HEADS, RW_HD, RW_HD)


def _dup_heads(k):
    k0, k1 = k[..., :SWA_HD], k[..., SWA_HD:]
    return jnp.concatenate([k0, k0, k1, k1], axis=-1)


def _token_tile(n, cap):
    tm = min(n, cap)
    while n % tm:
        tm //= 2
    return tm


def kernel(x_prompt, x_sample, cache_swa_k, cache_swa_v, state_gla, state_rwkv, state_shift, cache_mem_k, cache_mem_v, mem_prompt, rel_bias, norm_mix, norm_xattn, norm_ffn, norm_mem, norm_final, ab_w_in, gla_w_a2, gla_b_a, gla_onorm, swa_sink, ab_w_out, rw_mu, rw_wr, rw_wk, rw_wv, rw_wo, rw_w0, rw_w1, rw_w2, rw_a0, rw_a1, rw_a2, rw_g1, rw_g2, rw_kk, rw_ka, rw_rk, rw_lnx_w, rw_lnx_b, xa_wq, xa_wk, xa_wv, xa_wo, ffn_w1, ffn_w3, ffn_w2):
    D = D_MODEL
    n_even = ab_w_in.shape[0]
    n_odd = rw_wr.shape[0]
    depth = xa_wq.shape[0]

    o_q, o_k, o_v, o_g, o_lr = 0, GLA_QK, 2 * GLA_QK, 2 * GLA_QK + GLA_V, 2 * GLA_QK + 2 * GLA_V
    o_sq = o_lr + GLA_LR
    o_sk = o_sq + SWA_HEADS * SWA_HD
    o_sv = o_sk + SWA_KV_HEADS * SWA_HD
    even_w = []
    for i in range(n_even):
        w = ab_w_in[i]
        wsk = w[:, o_sk:o_sv]
        wsv = w[:, o_sv:o_sv + SWA_KV_HEADS * SWA_HD]
        wlr = jnp.pad(w[:, o_lr:o_sq], ((0, 0), (0, LANES - GLA_LR)))
        w_all = jnp.concatenate([
            w[:, o_q:o_k] * (GLA_DK ** -0.5), w[:, o_k:o_v], w[:, o_v:o_g], w[:, o_g:o_lr],
            w[:, o_sq:o_sk] * (SWA_HD ** -0.5), wsk, wsv, _dup_heads(wsk), _dup_heads(wsv), wlr], axis=1).astype(BF16)
        even_w.append(dict(
            w_all=w_all,
            wa2p=jnp.pad(gla_w_a2[i], ((0, LANES - GLA_LR), (0, 0))).astype(BF16),
            wo_a=ab_w_out[i][:GLA_V].astype(BF16), wo_b=ab_w_out[i][GLA_V:].astype(BF16)))
    even_splits = (GLA_QK, GLA_QK, GLA_V, GLA_V, SWA_HEADS * SWA_HD, 2 * SWA_HD, 2 * SWA_HD, 4 * SWA_HD, 4 * SWA_HD, LANES)
    even_dtypes = (BF16, BF16, BF16, BF16, BF16, F32, F32, BF16, BF16, F32)
    odd_w = []
    for i in range(n_odd):
        odd_w.append(dict(
            mu=rw_mu[i], wr=rw_wr[i].astype(BF16), wk=rw_wk[i].astype(BF16), wv=rw_wv[i].astype(BF16),
            w1=rw_w1[i].astype(BF16), w2=rw_w2[i].astype(BF16), a1=rw_a1[i].astype(BF16), a2=rw_a2[i].astype(BF16),
            g1=rw_g1[i].astype(BF16), g2=rw_g2[i].astype(BF16), w0=rw_w0[i], a0=rw_a0[i], k_k=rw_kk[i], k_a=rw_ka[i],
            wo=rw_wo[i].astype(BF16)))
    wq_b = [(xa_wq[l] * (X_HD ** -0.5)).astype(BF16) for l in range(depth)]
    wo_b = [xa_wo[l].astype(BF16) for l in range(depth)]
    wkv_b = [jnp.concatenate([xa_wk[l], xa_wv[l]], axis=1).astype(BF16) for l in range(depth)]
    f1 = [ffn_w1[l].astype(BF16) for l in range(depth)]
    f3 = [ffn_w3[l].astype(BF16) for l in range(depth)]
    f2 = [ffn_w2[l].astype(BF16) for l in range(depth)]
    gi = np.arange(256) // RW_HD
    ones_bd = jnp.asarray((gi[:, None] == gi[None, :]).astype(np.float32)).astype(BF16)

    def run(x, mem_k, mem_v, swa_k, swa_v, gla_s, rw_s, rw_shift):
        prompt = gla_s is None
        B, T, _ = x.shape
        n = B * T
        L = min(CHUNK, T)
        tq = min(WINDOW, T)
        tm = _token_tile(n, 512)
        tmb = _token_tile(T, 512)
        tm_ffn = _token_tile(n, 256)
        bias = swa_bias(rel_bias, tq, WINDOW + tq, chunked=prompt)
        out_k, out_v, out_gla, out_rw, out_shift = [], [], [], [], []
        x2 = x.reshape(n, D)
        y2 = None
        for l in range(depth):
            i = l // 2
            if l % 2 == 0:
                ew = even_w[i]
                q, k, v, gate, sq, sk, sv, kdup, vdup, lr = norm_proj(x2, norm_mix[l], ew["w_all"], even_splits, even_dtypes, tm)
                s0t = jnp.zeros((B, GLA_HEADS // 2, GLA_DV, 2 * GLA_DK), F32) if prompt else _gla_state_in(gla_s[i])
                o_a, st = gla_mixer(q.reshape(B, T, -1), k.reshape(B, T, -1), v.reshape(B, T, -1), gate.reshape(B, T, -1),
                                    lr.reshape(B, T, -1), ew["wa2p"], gla_b_a[i], gla_onorm[i], s0t, L)
                sk3 = sk.reshape(B, T, 2 * SWA_HD)
                sv3 = sv.reshape(B, T, 2 * SWA_HD)
                if prompt:
                    pre_k = jnp.zeros((B, WINDOW, 4 * SWA_HD), BF16)
                    pre_v = pre_k
                    new_k, new_v = sk3[:, -WINDOW:], sv3[:, -WINDOW:]
                else:
                    ck = swa_k[i].reshape(B, WINDOW, 2 * SWA_HD)
                    cv = swa_v[i].reshape(B, WINDOW, 2 * SWA_HD)
                    pre_k = _dup_heads(ck).astype(BF16)
                    pre_v = _dup_heads(cv).astype(BF16)
                    new_k = jnp.concatenate([ck, sk3], axis=1)[:, -WINDOW:]
                    new_v = jnp.concatenate([cv, sv3], axis=1)[:, -WINDOW:]
                k_ext = jnp.concatenate([pre_k, kdup.reshape(B, T, -1)], axis=1)
                v_ext = jnp.concatenate([pre_v, vdup.reshape(B, T, -1)], axis=1)
                o_b = swa_mixer(sq.reshape(B, T, -1), k_ext, v_ext, bias, swa_sink[i], tq, prefix_valid=not prompt)
                x2 = outproj_residual(x2, o_a.reshape(n, -1), o_b.reshape(n, -1), ew["wo_a"], ew["wo_b"], tm)
                out_k.append(new_k.reshape(B, WINDOW, SWA_KV_HEADS, SWA_HD))
                out_v.append(new_v.reshape(B, WINDOW, SWA_KV_HEADS, SWA_HD))
                out_gla.append(_gla_state_out(st))
            else:
                ow = odd_w[i]
                shift0 = jnp.zeros((B, D), F32) if prompt else rw_shift[i]
                s0 = jnp.zeros((B, RW_HEADS // 2, HEAD_PAIR, HEAD_PAIR), F32) if prompt else _rw_state_in(rw_s[i])
                r, lw, k2, v, kn, ba, gg, hl = rwkv_pre(x2.reshape(B, T, D), shift0, norm_mix[l], ow, ones_bd, tmb)
                y, sb = rwkv_scan(r, lw, k2, v, kn, ba, s0, L)
                x2 = rwkv_post(x2, y.reshape(n, D), r.reshape(n, D), k2.reshape(n, D), v.reshape(n, D), gg.reshape(n, D),
                               rw_lnx_w[i], rw_lnx_b[i], rw_rk[i].reshape(D), ones_bd, ow["wo"], tm)
                out_rw.append(_rw_state_out(sb))
                out_shift.append(hl.reshape(B, D))
            mk = mem_k[l].reshape(B, -1, D).astype(BF16)
            mv = mem_v[l].reshape(B, -1, D).astype(BF16)
            x2 = xattn_residual(x2.reshape(B, T, D), norm_xattn[l], wq_b[l], mk, mv, wo_b[l], tmb).reshape(n, D)
            res = ffn_residual(x2, norm_ffn[l], f1[l], f3[l], f2[l], norm_final, tm_ffn, with_final=(l == depth - 1))
            x2 = res[0]
            y2 = res[-1]
        return (y2.reshape(B, T, D), jnp.stack(out_k), jnp.stack(out_v), jnp.stack(out_gla), jnp.stack(out_rw),
                jnp.stack(out_shift))

    Bp, M, _ = mem_prompt.shape
    mem2 = mem_prompt.reshape(Bp * M, D)
    pk, pv = [], []
    for l in range(depth):
        kk_, vv_ = norm_proj(mem2, norm_mem[l], wkv_b[l], (D, D), (F32, F32), _token_tile(Bp * M, 512))
        pk.append(kk_.reshape(Bp, M, X_HEADS, X_HD))
        pv.append(vv_.reshape(Bp, M, X_HEADS, X_HD))
    p_mem_k = jnp.stack(pk)
    p_mem_v = jnp.stack(pv)
    y_prompt, p_swa_k, p_swa_v, p_gla, p_rwkv, p_shift = run(x_prompt, p_mem_k, p_mem_v, None, None, None, None, None)
    y_sample, s_swa_k, s_swa_v, s_gla, s_rwkv, s_shift = run(
        x_sample, cache_mem_k, cache_mem_v, cache_swa_k, cache_swa_v, state_gla, state_rwkv, state_shift)
    return (y_prompt, y_sample, p_swa_k, p_swa_v, p_gla, p_rwkv, p_shift, p_mem_k, p_mem_v,
            s_swa_k, s_swa_v, s_gla, s_rwkv, s_shift)
```

```python
import functools
import math

import jax
import jax.numpy as jnp
import numpy as np
from jax import lax
from jax.experimental import pallas as pl
from jax.experimental.pallas import tpu as pltpu

F32 = jnp.float32
BF16 = jnp.bfloat16

D_MODEL = 1024
DEPTH = 4
CHUNK = 64
NORM_EPS = 1e-6
NEG_INF = -1e30
GLA_HEADS = 4
GLA_DK = 64
GLA_DV = 128
GLA_QK = GLA_HEADS * GLA_DK
GLA_V = GLA_HEADS * GLA_DV
GLA_LR = 16
GLA_TAU = 16.0
SWA_HEADS = 8
SWA_KV_HEADS = 2
SWA_GROUP = SWA_HEADS // SWA_KV_HEADS
SWA_HD = 64
WINDOW = 128
N_BUCKETS = 32
MAX_DIST = 128
RW_HD = 64
RW_HEADS = D_MODEL // RW_HD
RW_LN_EPS = 64e-5
X_HEADS = 4
X_HD = D_MODEL // X_HEADS

LANES = 128
HEAD_PAIR = 2 * RW_HD
VMEM_LIMIT = 56 * 1024 * 1024
GLA_CHUNKS_PER_STEP = 4
RW_CHUNKS_PER_STEP = 2


def _cparams(*sem):
    return pltpu.CompilerParams(dimension_semantics=sem, vmem_limit_bytes=VMEM_LIMIT)


def _dot(a, b):
    return jnp.dot(a, b, preferred_element_type=F32)


def _dot_nt(a, b):
    return lax.dot_general(a, b, (((1,), (1,)), ((), ())), preferred_element_type=F32)


def _dot_tn(a, b):
    return lax.dot_general(a, b, (((0,), (0,)), ((), ())), preferred_element_type=F32)


def _rms(x, g, eps):
    return x * lax.rsqrt(jnp.mean(x * x, axis=-1, keepdims=True) + eps) * g


def _sigmoid(x):
    return 1.0 / (1.0 + jnp.exp(-x))


def _log_sigmoid(x):
    return jnp.minimum(x, 0.0) - jnp.log(1.0 + jnp.exp(-jnp.abs(x)))


def _cumsum_rows(x, chunk):
    n = x.shape[0]
    r = lax.broadcasted_iota(jnp.int32, (n, n), 0)
    c = lax.broadcasted_iota(jnp.int32, (n, n), 1)
    assert chunk & (chunk - 1) == 0
    tri = jnp.where(c <= r, jnp.where(c >= jnp.bitwise_and(r, -chunk), 1.0, 0.0), 0.0).astype(BF16)
    hi = x.astype(BF16)
    r1 = x - hi.astype(F32)
    mid = r1.astype(BF16)
    lo = (r1 - mid.astype(F32)).astype(BF16)
    return _dot(tri, hi) + _dot(tri, mid) + _dot(tri, lo)


def _head0_mask(width, half):
    return lax.broadcasted_iota(jnp.int32, (1, width), 1) < half


def _bd(x, m0):
    zero = jnp.zeros_like(x)
    return jnp.concatenate([jnp.where(m0, x, zero), jnp.where(m0, zero, x)], axis=0)


def _seg_sum(x, ones_bd):
    parts = []
    for j in range(x.shape[1] // 256):
        parts.append(_dot(x[:, 256 * j:256 * (j + 1)].astype(BF16), ones_bd))
    return jnp.concatenate(parts, axis=1)


def _full(shape):
    nd = len(shape)
    return pl.BlockSpec(shape, lambda *_: (0,) * nd, pipeline_mode=pl.Buffered(1))


def _norm_proj_kernel(x_ref, g_ref, w_ref, *out_refs, splits, eps):
    h = _rms(x_ref[...], g_ref[...], eps).astype(BF16)
    p = _dot(h, w_ref[...])
    off = 0
    for o_ref, width in zip(out_refs, splits):
        o_ref[...] = p[:, off:off + width].astype(o_ref.dtype)
        off += width


def norm_proj(x2, g, w, splits, dtypes, tm):
    n, d = x2.shape
    ntot = w.shape[1]
    assert sum(splits) == ntot and n % tm == 0
    return pl.pallas_call(
        functools.partial(_norm_proj_kernel, splits=tuple(splits), eps=NORM_EPS),
        grid=(n // tm,),
        in_specs=[pl.BlockSpec((tm, d), lambda i: (i, 0)), _full((1, d)), _full((d, ntot))],
        out_specs=[pl.BlockSpec((tm, s), lambda i: (i, 0)) for s in splits],
        out_shape=[jax.ShapeDtypeStruct((n, s), dt) for s, dt in zip(splits, dtypes)],
        compiler_params=_cparams("parallel"),
        name="norm_proj",
    )(x2, g.reshape(1, d), w)


def _outproj_kernel(x_ref, a_ref, b_ref, wa_ref, wb_ref, o_ref):
    o_ref[...] = x_ref[...] + _dot(a_ref[...], wa_ref[...]) + _dot(b_ref[...], wb_ref[...])


def outproj_residual(x2, a, b, wa, wb, tm):
    n, d = x2.shape
    ka, kb = a.shape[1], b.shape[1]
    return pl.pallas_call(
        _outproj_kernel,
        grid=(n // tm,),
        in_specs=[pl.BlockSpec((tm, d), lambda i: (i, 0)), pl.BlockSpec((tm, ka), lambda i: (i, 0)),
                  pl.BlockSpec((tm, kb), lambda i: (i, 0)), _full((ka, d)), _full((kb, d))],
        out_specs=pl.BlockSpec((tm, d), lambda i: (i, 0)),
        out_shape=jax.ShapeDtypeStruct((n, d), F32),
        compiler_params=_cparams("parallel"),
        name="outproj_residual",
    )(x2, a, b, wa, wb)


def _xattn_kernel(x_ref, g_ref, wq_ref, mk_ref, mv_ref, wo_ref, o_ref, *, eps):
    x = x_ref[0]
    h = _rms(x, g_ref[...], eps).astype(BF16)
    q = _dot(h, wq_ref[...]).astype(BF16)
    outs = []
    for hh in range(X_HEADS):
        sl = slice(X_HD * hh, X_HD * (hh + 1))
        s = _dot_nt(q[:, sl], mk_ref[0, :, sl])
        m = jnp.max(s, axis=-1, keepdims=True)
        ex = jnp.exp(s - m)
        den = jnp.sum(ex, axis=-1, keepdims=True)
        outs.append((_dot(ex.astype(BF16), mv_ref[0, :, sl]) / den).astype(BF16))
    o = jnp.concatenate(outs, axis=1)
    o_ref[0] = x + _dot(o, wo_ref[...])


def xattn_residual(x, g, wq, mk, mv, wo, tm):
    b, t, d = x.shape
    m = mk.shape[1]
    return pl.pallas_call(
        functools.partial(_xattn_kernel, eps=NORM_EPS),
        grid=(b, t // tm),
        in_specs=[pl.BlockSpec((1, tm, d), lambda i, j: (i, j, 0)), _full((1, d)), _full((d, d)),
                  pl.BlockSpec((1, m, d), lambda i, j: (i, 0, 0)), pl.BlockSpec((1, m, d), lambda i, j: (i, 0, 0)),
                  _full((d, d))],
        out_specs=pl.BlockSpec((1, tm, d), lambda i, j: (i, j, 0)),
        out_shape=jax.ShapeDtypeStruct((b, t, d), F32),
        compiler_params=_cparams("parallel", "parallel"),
        name="xattn_residual",
    )(x, g.reshape(1, d), wq, mk, mv, wo)


def _ffn_kernel(x_ref, g_ref, w1_ref, w3_ref, w2_ref, gf_ref, o_ref, *y_ref, eps):
    x = x_ref[...]
    h = _rms(x, g_ref[...], eps).astype(BF16)
    a = _dot(h, w1_ref[...])
    b = _dot(h, w3_ref[...])
    u = (a * _sigmoid(a) * b).astype(BF16)
    o = x + _dot(u, w2_ref[...])
    o_ref[...] = o
    if y_ref:
        y_ref[0][...] = _rms(o, gf_ref[...], eps)


def ffn_residual(x2, g, w1, w3, w2, gf, tm, with_final):
    n, d = x2.shape
    f = w1.shape[1]
    tok = pl.BlockSpec((tm, d), lambda i: (i, 0))
    n_out = 2 if with_final else 1
    return pl.pallas_call(
        functools.partial(_ffn_kernel, eps=NORM_EPS),
        grid=(n // tm,),
        in_specs=[tok, _full((1, d)), _full((d, f)), _full((d, f)), _full((f, d)), _full((1, d))],
        out_specs=[tok] * n_out,
        out_shape=[jax.ShapeDtypeStruct((n, d), F32)] * n_out,
        compiler_params=_cparams("parallel"),
        name="ffn_residual",
    )(x2, g.reshape(1, d), w1, w3, w2, gf.reshape(1, d))


def _gla_kernel(q_ref, k_ref, v_ref, gate_ref, lr_ref, wa2_ref, ba_ref, on_ref, s0_ref, o_ref, sout_ref, s_scr, *, L, nc, eps):
    c = pl.program_id(1)

    @pl.when(c == 0)
    def _():
        s_scr[...] = s0_ref[0]

    z = _dot(lr_ref[0].astype(BF16), wa2_ref[...]) + ba_ref[...]
    b_all = _cumsum_rows(_log_sigmoid(z) * (1.0 / GLA_TAU), L)
    m0 = _head0_mask(HEAD_PAIR, GLA_DK)
    m0s = _head0_mask(2 * L, L)
    not0, not0s = jnp.logical_not(m0), jnp.logical_not(m0s)
    rr = lax.broadcasted_iota(jnp.int32, (L, 2 * L), 0)
    cc = lax.broadcasted_iota(jnp.int32, (L, 2 * L), 1)
    causal = jnp.where(cc >= L, cc - L, cc) <= rr
    pairs = range(GLA_HEADS // 2)
    sls = [slice(HEAD_PAIR * p, HEAD_PAIR * (p + 1)) for p in pairs]
    qbs, decs, o_intra, ds = [], [], [], []
    for j in range(nc):
        rows = slice(j * L, (j + 1) * L)
        b = b_all[rows]
        mrow = b[L // 2:L // 2 + 1]
        blast = b[L - 1:L]
        qd = q_ref[0, rows, :].astype(F32) * jnp.exp(b - mrow)
        kd = k_ref[0, rows, :].astype(F32) * jnp.exp(mrow - b)
        qbs.append((qd * jnp.exp(mrow)).astype(BF16))
        kl = (kd * jnp.exp(blast - mrow)).astype(BF16)
        decs.append(jnp.exp(blast))
        v = v_ref[0, rows, :]
        oj, dj = [], []
        for p in pairs:
            att = _dot_nt(qd[:, sls[p]].astype(BF16), _bd(kd[:, sls[p]].astype(BF16), m0))
            att = jnp.where(causal, att, 0.0).astype(BF16)
            vst = jnp.concatenate([v[:, GLA_DV * (2 * p):GLA_DV * (2 * p + 1)],
                                   v[:, GLA_DV * (2 * p + 1):GLA_DV * (2 * p + 2)]], axis=0)
            zero = jnp.zeros_like(att)
            oj.append([_dot(jnp.where(m0s, att, zero), vst), _dot(jnp.where(not0s, att, zero), vst)])
            kl_p = kl[:, sls[p]]
            kz = jnp.zeros_like(kl_p)
            dj.append(_dot_tn(v[:, GLA_DV * (2 * p):GLA_DV * (2 * p + 1)], jnp.where(m0, kl_p, kz))
                      + _dot_tn(v[:, GLA_DV * (2 * p + 1):GLA_DV * (2 * p + 2)], jnp.where(not0, kl_p, kz)))
        o_intra.append(oj)
        ds.append(dj)
    sts = [s_scr[p] for p in pairs]
    for j in range(nc):
        rows = slice(j * L, (j + 1) * L)
        for p in pairs:
            st_b = sts[p].astype(BF16)
            qb_p = qbs[j][:, sls[p]]
            qz = jnp.zeros_like(qb_p)
            for e in range(2):
                hs = slice(GLA_DV * (2 * p + e), GLA_DV * (2 * p + e + 1))
                o = o_intra[j][p][e] + _dot_nt(jnp.where(m0 if e == 0 else not0, qb_p, qz), st_b)
                o = _rms(o, on_ref[...], eps)
                g = gate_ref[0, rows, hs].astype(F32)
                o_ref[0, rows, hs] = (o * (g * _sigmoid(g))).astype(o_ref.dtype)
            sts[p] = sts[p] * decs[j][:, sls[p]] + ds[j][p]
    for p in pairs:
        s_scr[p] = sts[p]

    @pl.when(c == pl.num_programs(1) - 1)
    def _():
        sout_ref[0] = s_scr[...]


def gla_mixer(q, k, v, gate, lr, wa2p, ba, onorm, s0t, L, nc):
    b, t, _ = q.shape
    hp = GLA_HEADS // 2

    def tok(width):
        return pl.BlockSpec((1, nc * L, width), lambda i, j: (i, j, 0))

    st_spec = pl.BlockSpec((1, hp, GLA_DV, HEAD_PAIR), lambda i, j: (i, 0, 0, 0))
    return pl.pallas_call(
        functools.partial(_gla_kernel, L=L, nc=nc, eps=NORM_EPS),
        grid=(b, t // (nc * L)),
        in_specs=[tok(GLA_QK), tok(GLA_QK), tok(GLA_V), tok(GLA_V), tok(LANES), _full((LANES, GLA_QK)),
                  _full((1, GLA_QK)), _full((1, GLA_DV)), st_spec],
        out_specs=[tok(GLA_V), st_spec],
        out_shape=[jax.ShapeDtypeStruct((b, t, GLA_V), BF16), jax.ShapeDtypeStruct((b, hp, GLA_DV, HEAD_PAIR), F32)],
        scratch_shapes=[pltpu.VMEM((hp, GLA_DV, HEAD_PAIR), F32)],
        compiler_params=_cparams("parallel", "arbitrary"),
        name="gla_mixer",
    )(q, k, v, gate, lr, wa2p, ba.reshape(1, GLA_QK), onorm.reshape(1, GLA_DV), s0t)


def _bias_kernel(tab_ref, bucket_ref, valid_ref, o_ref):
    bk = bucket_ref[...]
    ok = valid_ref[...] != 0
    for h in range(SWA_HEADS):
        acc = jnp.zeros(bk.shape, F32)
        for n in range(N_BUCKETS):
            acc = jnp.where(bk == n, tab_ref[n, h], acc)
        o_ref[h] = jnp.where(ok, acc, NEG_INF)


def swa_bias(table, tq, lk, chunked):
    rel = jnp.arange(lk)[None, :] - WINDOW - jnp.arange(tq)[:, None]
    nb = N_BUCKETS // 2
    max_exact = nb // 2
    n = jnp.abs(rel)
    nf = jnp.maximum(n, 1).astype(F32)
    large = max_exact + (jnp.log(nf / max_exact) / math.log(MAX_DIST / max_exact) * (nb - max_exact)).astype(jnp.int32)
    large = jnp.minimum(large, nb - 1)
    bucket = (jnp.where(rel > 0, nb, 0) + jnp.where(n < max_exact, n, large)).astype(jnp.int32)
    if chunked:
        qc = np.arange(tq)[:, None] // CHUNK
        kc = np.arange(lk)[None, :] // CHUNK
        valid = ((kc >= qc) & (kc <= qc + WINDOW // CHUNK)).astype(np.int32)
    else:
        valid = np.ones((tq, lk), np.int32)
    return pl.pallas_call(
        _bias_kernel,
        in_specs=[pl.BlockSpec(memory_space=pltpu.SMEM), _full((tq, lk)), _full((tq, lk))],
        out_specs=_full((SWA_HEADS, tq, lk)),
        out_shape=jax.ShapeDtypeStruct((SWA_HEADS, tq, lk), F32),
        grid=(1,),
        name="swa_bias",
    )(table.astype(F32), bucket, jnp.asarray(valid))


def _swa_kernel(sink_ref, q_ref, k_ref, v_ref, bias_ref, o_ref, *, tq, lk, prefix_valid):
    c = pl.program_id(1)
    start = pl.multiple_of(c * tq, tq)
    kk = k_ref[0, pl.ds(start, lk), :]
    vv = v_ref[0, pl.ds(start, lk), :]
    q = q_ref[0]
    m0 = _head0_mask(HEAD_PAIR, SWA_HD)
    if not prefix_valid:
        key_ok = (lax.broadcasted_iota(jnp.int32, (tq, lk), 1) + start) >= WINDOW
    for p in range(SWA_HEADS // 2):
        j = (2 * p) // SWA_GROUP
        qp = q[:, HEAD_PAIR * p:HEAD_PAIR * (p + 1)]
        kj = kk[:, HEAD_PAIR * j:HEAD_PAIR * (j + 1)]
        vj = vv[:, HEAD_PAIR * j:HEAD_PAIR * (j + 1)]
        outs = []
        for e in range(2):
            h = 2 * p + e
            me = m0 if e == 0 else jnp.logical_not(m0)
            s = _dot_nt(jnp.where(me, qp, jnp.zeros_like(qp)), kj) + bias_ref[h]
            if not prefix_valid:
                s = jnp.where(key_ok, s, NEG_INF)
            snk = sink_ref[h]
            m = jnp.maximum(jnp.max(s, axis=-1, keepdims=True), snk)
            ex = jnp.exp(s - m)
            den = jnp.sum(ex, axis=-1, keepdims=True) + jnp.exp(snk - m)
            outs.append(_dot(ex.astype(BF16), vj) / den)
        o_ref[0, :, HEAD_PAIR * p:HEAD_PAIR * (p + 1)] = jnp.where(m0, outs[0], outs[1]).astype(o_ref.dtype)


def swa_mixer(sq, kdup_ext, vdup_ext, bias, sink, tq, prefix_valid):
    b, t, c = sq.shape
    text = kdup_ext.shape[1]
    lk = WINDOW + tq
    return pl.pallas_call(
        functools.partial(_swa_kernel, tq=tq, lk=lk, prefix_valid=prefix_valid),
        grid=(b, t // tq),
        in_specs=[pl.BlockSpec(memory_space=pltpu.SMEM),
                  pl.BlockSpec((1, tq, c), lambda i, j: (i, j, 0)),
                  pl.BlockSpec((1, text, 2 * HEAD_PAIR), lambda i, j: (i, 0, 0)),
                  pl.BlockSpec((1, text, 2 * HEAD_PAIR), lambda i, j: (i, 0, 0)),
                  _full((SWA_HEADS, tq, lk))],
        out_specs=pl.BlockSpec((1, tq, c), lambda i, j: (i, j, 0)),
        out_shape=jax.ShapeDtypeStruct((b, t, c), BF16),
        compiler_params=_cparams("parallel", "parallel"),
        name="swa_mixer",
    )(sink.astype(F32), sq, kdup_ext, vdup_ext, bias)


def _rwkv_pre_kernel(x_ref, sh_ref, g_ref, mu_ref, wr_ref, wk_ref, wv_ref, w1_ref, w2_ref, a1_ref, a2_ref, g1_ref,
                     g2_ref, w0_ref, a0_ref, kk_ref, ka_ref, ones_ref,
                     r_o, lw_o, k_o, v_o, kn_o, ba_o, gg_o, hl_o, carry, *, tm, eps):
    t = pl.program_id(1)

    @pl.when(t == 0)
    def _():
        carry[...] = sh_ref[0]

    h = _rms(x_ref[0], g_ref[...], eps)
    row = lax.broadcasted_iota(jnp.int32, h.shape, 0)
    prev = jnp.where(row == 0, carry[...], pltpu.roll(h, 1, axis=0))
    last = h[tm - 1:tm]
    carry[...] = last
    hl_o[0] = last
    d = prev - h

    def mix(i):
        return (h + d * mu_ref[i:i + 1]).astype(BF16)

    r = _dot(mix(0), wr_ref[...])
    zw = w0_ref[...] + _dot(jnp.tanh(_dot(mix(1), w1_ref[...])).astype(BF16), w2_ref[...])
    k = _dot(mix(2), wk_ref[...])
    v = _dot(mix(3), wv_ref[...])
    a = _sigmoid(a0_ref[...] + _dot(_dot(mix(4), a1_ref[...]).astype(BF16), a2_ref[...]))
    gg = _dot(_sigmoid(_dot(mix(5), g1_ref[...])).astype(BF16), g2_ref[...])
    lw = -math.exp(-0.5) * _sigmoid(zw)
    kk = k * kk_ref[...]
    nrm = jnp.sqrt(_seg_sum(kk * kk, ones_ref[...]))
    kn = kk / jnp.maximum(nrm, 1e-12)
    k2 = k * (1.0 + (a - 1.0) * ka_ref[...])
    r_o[0] = r.astype(r_o.dtype)
    lw_o[0] = lw
    k_o[0] = k2.astype(k_o.dtype)
    v_o[0] = v.astype(v_o.dtype)
    kn_o[0] = kn.astype(kn_o.dtype)
    ba_o[0] = (kn * a).astype(ba_o.dtype)
    gg_o[0] = gg.astype(gg_o.dtype)


def rwkv_pre(x, shift0, g, lw_, ones_bd, tm):
    b, t, d = x.shape
    tok = pl.BlockSpec((1, tm, d), lambda i, j: (i, j, 0))
    row = pl.BlockSpec((1, 1, d), lambda i, j: (i, 0, 0))
    vec = _full((1, d))
    mats = [lw_["wr"], lw_["wk"], lw_["wv"], lw_["w1"], lw_["w2"], lw_["a1"], lw_["a2"], lw_["g1"], lw_["g2"]]
    vecs = [lw_["w0"], lw_["a0"], lw_["k_k"], lw_["k_a"]]
    outs = pl.pallas_call(
        functools.partial(_rwkv_pre_kernel, tm=tm, eps=NORM_EPS),
        grid=(b, t // tm),
        in_specs=[tok, row, vec, _full((6, d))] + [_full(m.shape) for m in mats] + [vec] * 4 + [_full((256, 256))],
        out_specs=[tok] * 7 + [row],
        out_shape=[jax.ShapeDtypeStruct((b, t, d), BF16), jax.ShapeDtypeStruct((b, t, d), F32)]
        + [jax.ShapeDtypeStruct((b, t, d), BF16)] * 5 + [jax.ShapeDtypeStruct((b, 1, d), F32)],
        scratch_shapes=[pltpu.VMEM((1, d), F32)],
        compiler_params=_cparams("parallel", "arbitrary"),
        name="rwkv_pre",
    )(x, shift0.reshape(b, 1, d), g.reshape(1, d), lw_["mu"], *mats, *[u.reshape(1, d) for u in vecs], ones_bd)
    return outs


def _rwkv_scan_kernel(r_ref, lw_ref, k_ref, v_ref, kn_ref, ba_ref, s0_ref, y_ref, sout_ref, s_scr, *, L, nc):
    c = pl.program_id(1)

    @pl.when(c == 0)
    def _():
        s_scr[...] = s0_ref[0]

    lw_all = lw_ref[0]
    pc_all = _cumsum_rows(lw_all, L)
    rt, at, bt, kt, bl, kl, dec = [], [], [], [], [], [], []
    for j in range(nc):
        rows = slice(j * L, (j + 1) * L)
        pc = pc_all[rows]
        plast = pc[L - 1:L]
        e_n = jnp.exp(-pc)
        e_l = jnp.exp(plast - pc)
        rt.append((r_ref[0, rows, :].astype(F32) * jnp.exp(pc)).astype(BF16))
        at.append((-kn_ref[0, rows, :].astype(F32) * jnp.exp(pc - lw_all[rows])).astype(BF16))
        ba = ba_ref[0, rows, :].astype(F32)
        kf = k_ref[0, rows, :].astype(F32)
        bt.append((ba * e_n).astype(BF16))
        kt.append((kf * e_n).astype(BF16))
        bl.append((ba * e_l).astype(BF16))
        kl.append((kf * e_l).astype(BF16))
        dec.append(jnp.exp(plast))
    m0 = _head0_mask(HEAD_PAIR, RW_HD)
    m0s = _head0_mask(2 * L, L)
    rr = lax.broadcasted_iota(jnp.int32, (L, 2 * L), 0)
    cc = lax.broadcasted_iota(jnp.int32, (L, 2 * L), 1)
    cm = jnp.where(cc >= L, cc - L, cc)
    strict = cm < rr
    incl = cm <= rr
    eye = jnp.where(cm == rr, 1.0, 0.0)
    br = lax.broadcasted_iota(jnp.int32, (HEAD_PAIR, HEAD_PAIR), 0)
    bc = lax.broadcasted_iota(jnp.int32, (HEAD_PAIR, HEAD_PAIR), 1)
    blockdiag = (br < RW_HD) == (bc < RW_HD)
    nsq = max(1, int(math.ceil(math.log2(L))))
    n_pairs = RW_HEADS // 2
    units = [(j, p) for j in range(nc) for p in range(n_pairs)]
    sls = [slice(HEAD_PAIR * p, HEAD_PAIR * (p + 1)) for p in range(n_pairs)]
    vs = [v_ref[0, j * L:(j + 1) * L, :] for j in range(nc)]
    scs = [_dot_nt(jnp.concatenate([at[j][:, sls[p]], rt[j][:, sls[p]]], axis=0),
                   jnp.concatenate([_bd(bt[j][:, sls[p]], m0), _bd(kt[j][:, sls[p]], m0)], axis=0))
           for j, p in units]
    a_ab = [jnp.where(strict, sc[:L, :2 * L], 0.0) for sc in scs]
    a_rb = [jnp.where(incl, sc[L:, :2 * L], 0.0).astype(BF16) for sc in scs]
    a_k = [jnp.concatenate([jnp.where(strict, sc[:L, 2 * L:], 0.0), jnp.where(incl, sc[L:, 2 * L:], 0.0)],
                           axis=0).astype(BF16) for sc in scs]
    akvs = [_dot(a_k[i], _bd(vs[j][:, sls[p]], m0)) for i, (j, p) in enumerate(units)]
    pws = a_ab
    tms = [eye + pw for pw in pws]
    for _ in range(1, nsq):
        pws = [_dot(pw.astype(BF16), _bd(pw.astype(BF16), m0s)) for pw in pws]
        tms = [t_ + _dot(t_.astype(BF16), _bd(pw.astype(BF16), m0s)) for t_, pw in zip(tms, pws)]
    wus = [_dot(tms[i].astype(BF16),
                jnp.concatenate([_bd(at[j][:, sls[p]], m0), _bd(akvs[i][:L].astype(BF16), m0)], axis=1)).astype(BF16)
           for i, (j, p) in enumerate(units)]
    qys = [_dot(a_rb[i], jnp.concatenate([_bd(wus[i][:, :HEAD_PAIR], m0), _bd(wus[i][:, HEAD_PAIR:], m0)], axis=1))
           for i in range(len(units))]
    qhs = [(rt[j][:, sls[p]].astype(F32) + qys[i][:, :HEAD_PAIR]).astype(BF16) for i, (j, p) in enumerate(units)]
    yvs = [qys[i][:, HEAD_PAIR:] + akvs[i][L:] for i in range(len(units))]
    mns = []
    for i, (j, p) in enumerate(units):
        v_p = vs[j][:, sls[p]]
        lhs = jnp.concatenate([wus[i], jnp.concatenate([jnp.zeros_like(v_p), v_p], axis=1)], axis=0)
        mn = _dot_tn(lhs, jnp.concatenate([bl[j][:, sls[p]], kl[j][:, sls[p]]], axis=0))
        mns.append((jnp.where(blockdiag, mn[:HEAD_PAIR], 0.0).astype(BF16), jnp.where(blockdiag, mn[HEAD_PAIR:], 0.0)))
    sts = [s_scr[p] for p in range(n_pairs)]
    for j in range(nc):
        rows = slice(j * L, (j + 1) * L)
        for p in range(n_pairs):
            i = j * n_pairs + p
            s_b = sts[p].astype(BF16)
            y_ref[0, rows, sls[p]] = (_dot_nt(qhs[i], s_b) + yvs[i]).astype(y_ref.dtype)
            sts[p] = sts[p] * dec[j][:, sls[p]] + _dot(s_b, mns[i][0]) + mns[i][1]
    for p in range(n_pairs):
        s_scr[p] = sts[p]

    @pl.when(c == pl.num_programs(1) - 1)
    def _():
        sout_ref[0] = s_scr[...]


def rwkv_scan(r, lw, k2, v, kn, ba, s0, L, nc):
    b, t, d = r.shape
    hp = RW_HEADS // 2
    tok = pl.BlockSpec((1, nc * L, d), lambda i, j: (i, j, 0))
    st_spec = pl.BlockSpec((1, hp, HEAD_PAIR, HEAD_PAIR), lambda i, j: (i, 0, 0, 0))
    return pl.pallas_call(
        functools.partial(_rwkv_scan_kernel, L=L, nc=nc),
        grid=(b, t // (nc * L)),
        in_specs=[tok] * 6 + [st_spec],
        out_specs=[tok, st_spec],
        out_shape=[jax.ShapeDtypeStruct((b, t, d), F32), jax.ShapeDtypeStruct((b, hp, HEAD_PAIR, HEAD_PAIR), F32)],
        scratch_shapes=[pltpu.VMEM((hp, HEAD_PAIR, HEAD_PAIR), F32)],
        compiler_params=_cparams("parallel", "arbitrary"),
        name="rwkv_scan",
    )(r, lw, k2, v, kn, ba, s0)


def _rwkv_post_kernel(x_ref, y_ref, r_ref, k_ref, v_ref, gg_ref, lnw_ref, lnb_ref, rk_ref, ones_ref, wo_ref, o_ref):
    ones = ones_ref[...]
    y = y_ref[...]
    inv = 1.0 / RW_HD
    mean = _seg_sum(y, ones) * inv
    yc = y - mean
    var = _seg_sum(yc * yc, ones) * inv
    yn = yc * lax.rsqrt(var + RW_LN_EPS) * lnw_ref[...] + lnb_ref[...]
    v = v_ref[...].astype(F32)
    bonus = _seg_sum(r_ref[...].astype(F32) * k_ref[...].astype(F32) * rk_ref[...], ones)
    z = ((yn + bonus * v) * gg_ref[...].astype(F32)).astype(BF16)
    o_ref[...] = x_ref[...] + _dot(z, wo_ref[...])


def rwkv_post(x2, y, r, k2, v, gg, lnw, lnb, rk, ones_bd, wo, tm):
    n, d = x2.shape
    tok = pl.BlockSpec((tm, d), lambda i: (i, 0))
    vec = _full((1, d))
    return pl.pallas_call(
        _rwkv_post_kernel,
        grid=(n // tm,),
        in_specs=[tok] * 6 + [vec] * 3 + [_full((256, 256)), _full((d, d))],
        out_specs=tok,
        out_shape=jax.ShapeDtypeStruct((n, d), F32),
        compiler_params=_cparams("parallel"),
        name="rwkv_post",
    )(x2, y, r, k2, v, gg, lnw.reshape(1, d), lnb.reshape(1, d), rk.reshape(1, d), ones_bd, wo)


def _gla_state_in(s):
    b = s.shape[0]
    st = jnp.swapaxes(s.astype(F32), 2, 3).reshape(b, GLA_HEADS // 2, 2, GLA_DV, GLA_DK)
    return jnp.transpose(st, (0, 1, 3, 2, 4)).reshape(b, GLA_HEADS // 2, GLA_DV, 2 * GLA_DK)


def _gla_state_out(st):
    b = st.shape[0]
    s = st.reshape(b, GLA_HEADS // 2, GLA_DV, 2, GLA_DK)
    s = jnp.transpose(s, (0, 1, 3, 2, 4)).reshape(b, GLA_HEADS, GLA_DV, GLA_DK)
    return jnp.swapaxes(s, 2, 3)


def _rw_state_in(s):
    b = s.shape[0]
    sp = s.astype(F32).reshape(b, RW_HEADS // 2, 2, RW_HD, RW_HD)
    z = jnp.zeros_like(sp[:, :, 0])
    top = jnp.concatenate([sp[:, :, 0], z], axis=-1)
    bot = jnp.concatenate([z, sp[:, :, 1]], axis=-1)
    return jnp.concatenate([top, bot], axis=-2)


def _rw_state_out(sb):
    b = sb.shape[0]
    s0 = sb[:, :, :RW_HD, :RW_HD]
    s1 = sb[:, :, RW_HD:, RW_HD:]
    return jnp.stack([s0, s1], axis=2).reshape(b, RW_HEADS, RW_HD, RW_HD)


def _dup_heads(k):
    k0, k1 = k[..., :SWA_HD], k[..., SWA_HD:]
    return jnp.concatenate([k0, k0, k1, k1], axis=-1)


def _token_tile(n, cap):
    tm = min(n, cap)
    while n % tm:
        tm //= 2
    return tm


def kernel(x_prompt, x_sample, cache_swa_k, cache_swa_v, state_gla, state_rwkv, state_shift, cache_mem_k, cache_mem_v, mem_prompt, rel_bias, norm_mix, norm_xattn, norm_ffn, norm_mem, norm_final, ab_w_in, gla_w_a2, gla_b_a, gla_onorm, swa_sink, ab_w_out, rw_mu, rw_wr, rw_wk, rw_wv, rw_wo, rw_w0, rw_w1, rw_w2, rw_a0, rw_a1, rw_a2, rw_g1, rw_g2, rw_kk, rw_ka, rw_rk, rw_lnx_w, rw_lnx_b, xa_wq, xa_wk, xa_wv, xa_wo, ffn_w1, ffn_w3, ffn_w2):
    D = D_MODEL
    n_even = ab_w_in.shape[0]
    n_odd = rw_wr.shape[0]
    depth = xa_wq.shape[0]

    o_q, o_k, o_v, o_g, o_lr = 0, GLA_QK, 2 * GLA_QK, 2 * GLA_QK + GLA_V, 2 * GLA_QK + 2 * GLA_V
    o_sq = o_lr + GLA_LR
    o_sk = o_sq + SWA_HEADS * SWA_HD
    o_sv = o_sk + SWA_KV_HEADS * SWA_HD
    even_w = []
    for i in range(n_even):
        w = ab_w_in[i]
        wsk = w[:, o_sk:o_sv]
        wsv = w[:, o_sv:o_sv + SWA_KV_HEADS * SWA_HD]
        wlr = jnp.pad(w[:, o_lr:o_sq], ((0, 0), (0, LANES - GLA_LR)))
        w_all = jnp.concatenate([
            w[:, o_q:o_k] * (GLA_DK ** -0.5), w[:, o_k:o_v], w[:, o_v:o_g], w[:, o_g:o_lr],
            w[:, o_sq:o_sk] * (SWA_HD ** -0.5), wsk, wsv, _dup_heads(wsk), _dup_heads(wsv), wlr], axis=1).astype(BF16)
        even_w.append(dict(
            w_all=w_all,
            wa2p=jnp.pad(gla_w_a2[i], ((0, LANES - GLA_LR), (0, 0))).astype(BF16),
            wo_a=ab_w_out[i][:GLA_V].astype(BF16), wo_b=ab_w_out[i][GLA_V:].astype(BF16)))
    even_splits = (GLA_QK, GLA_QK, GLA_V, GLA_V, SWA_HEADS * SWA_HD, 2 * SWA_HD, 2 * SWA_HD, 4 * SWA_HD, 4 * SWA_HD, LANES)
    even_dtypes = (BF16, BF16, BF16, BF16, BF16, F32, F32, BF16, BF16, F32)
    odd_w = []
    for i in range(n_odd):
        odd_w.append(dict(
            mu=rw_mu[i], wr=rw_wr[i].astype(BF16), wk=rw_wk[i].astype(BF16), wv=rw_wv[i].astype(BF16),
            w1=rw_w1[i].astype(BF16), w2=rw_w2[i].astype(BF16), a1=rw_a1[i].astype(BF16), a2=rw_a2[i].astype(BF16),
            g1=rw_g1[i].astype(BF16), g2=rw_g2[i].astype(BF16), w0=rw_w0[i], a0=rw_a0[i], k_k=rw_kk[i], k_a=rw_ka[i],
            wo=rw_wo[i].astype(BF16)))
    wq_b = [(xa_wq[l] * (X_HD ** -0.5)).astype(BF16) for l in range(depth)]
    wo_b = [xa_wo[l].astype(BF16) for l in range(depth)]
    wkv_b = [jnp.concatenate([xa_wk[l], xa_wv[l]], axis=1).astype(BF16) for l in range(depth)]
    f1 = [ffn_w1[l].astype(BF16) for l in range(depth)]
    f3 = [ffn_w3[l].astype(BF16) for l in range(depth)]
    f2 = [ffn_w2[l].astype(BF16) for l in range(depth)]
    gi = np.arange(256) // RW_HD
    ones_bd = jnp.asarray((gi[:, None] == gi[None, :]).astype(np.float32)).astype(BF16)

    def run(x, mem_k, mem_v, swa_k, swa_v, gla_s, rw_s, rw_shift):
        prompt = gla_s is None
        B, T, _ = x.shape
        n = B * T
        L = min(CHUNK, T)
        nc_gla = _token_tile(T // L, GLA_CHUNKS_PER_STEP)
        nc_rw = _token_tile(T // L, RW_CHUNKS_PER_STEP)
        tq = min(WINDOW, T)
        tm = _token_tile(n, 512)
        tmb = _token_tile(T, 512)
        tm_ffn = _token_tile(n, 256)
        bias = swa_bias(rel_bias, tq, WINDOW + tq, chunked=prompt)
        out_k, out_v, out_gla, out_rw, out_shift = [], [], [], [], []
        x2 = x.reshape(n, D)
        y2 = None
        for l in range(depth):
            i = l // 2
            if l % 2 == 0:
                ew = even_w[i]
                q, k, v, gate, sq, sk, sv, kdup, vdup, lr = norm_proj(x2, norm_mix[l], ew["w_all"], even_splits, even_dtypes, tm)
                s0t = jnp.zeros((B, GLA_HEADS // 2, GLA_DV, 2 * GLA_DK), F32) if prompt else _gla_state_in(gla_s[i])
                o_a, st = gla_mixer(q.reshape(B, T, -1), k.reshape(B, T, -1), v.reshape(B, T, -1), gate.reshape(B, T, -1),
                                    lr.reshape(B, T, -1), ew["wa2p"], gla_b_a[i], gla_onorm[i], s0t, L, nc_gla)
                sk3 = sk.reshape(B, T, 2 * SWA_HD)
                sv3 = sv.reshape(B, T, 2 * SWA_HD)
                if prompt:
                    pre_k = jnp.zeros((B, WINDOW, 4 * SWA_HD), BF16)
                    pre_v = pre_k
                    new_k, new_v = sk3[:, -WINDOW:], sv3[:, -WINDOW:]
                else:
                    ck = swa_k[i].reshape(B, WINDOW, 2 * SWA_HD)
                    cv = swa_v[i].reshape(B, WINDOW, 2 * SWA_HD)
                    pre_k = _dup_heads(ck).astype(BF16)
                    pre_v = _dup_heads(cv).astype(BF16)
                    new_k = jnp.concatenate([ck, sk3], axis=1)[:, -WINDOW:]
                    new_v = jnp.concatenate([cv, sv3], axis=1)[:, -WINDOW:]
                k_ext = jnp.concatenate([pre_k, kdup.reshape(B, T, -1)], axis=1)
                v_ext = jnp.concatenate([pre_v, vdup.reshape(B, T, -1)], axis=1)
                o_b = swa_mixer(sq.reshape(B, T, -1), k_ext, v_ext, bias, swa_sink[i], tq, prefix_valid=not prompt)
                x2 = outproj_residual(x2, o_a.reshape(n, -1), o_b.reshape(n, -1), ew["wo_a"], ew["wo_b"], tm)
                out_k.append(new_k.reshape(B, WINDOW, SWA_KV_HEADS, SWA_HD))
                out_v.append(new_v.reshape(B, WINDOW, SWA_KV_HEADS, SWA_HD))
                out_gla.append(_gla_state_out(st))
            else:
                ow = odd_w[i]
                shift0 = jnp.zeros((B, D), F32) if prompt else rw_shift[i]
                s0 = jnp.zeros((B, RW_HEADS // 2, HEAD_PAIR, HEAD_PAIR), F32) if prompt else _rw_state_in(rw_s[i])
                r, lw, k2, v, kn, ba, gg, hl = rwkv_pre(x2.reshape(B, T, D), shift0, norm_mix[l], ow, ones_bd, tmb)
                y, sb = rwkv_scan(r, lw, k2, v, kn, ba, s0, L, nc_rw)
                x2 = rwkv_post(x2, y.reshape(n, D), r.reshape(n, D), k2.reshape(n, D), v.reshape(n, D), gg.reshape(n, D),
                               rw_lnx_w[i], rw_lnx_b[i], rw_rk[i].reshape(D), ones_bd, ow["wo"], tm)
                out_rw.append(_rw_state_out(sb))
                out_shift.append(hl.reshape(B, D))
            mk = mem_k[l].reshape(B, -1, D).astype(BF16)
            mv = mem_v[l].reshape(B, -1, D).astype(BF16)
            x2 = xattn_residual(x2.reshape(B, T, D), norm_xattn[l], wq_b[l], mk, mv, wo_b[l], tmb).reshape(n, D)
            res = ffn_residual(x2, norm_ffn[l], f1[l], f3[l], f2[l], norm_final, tm_ffn, with_final=(l == depth - 1))
            x2 = res[0]
            y2 = res[-1]
        return (y2.reshape(B, T, D), jnp.stack(out_k), jnp.stack(out_v), jnp.stack(out_gla), jnp.stack(out_rw),
                jnp.stack(out_shift))

    Bp, M, _ = mem_prompt.shape
    mem2 = mem_prompt.reshape(Bp * M, D)
    pk, pv = [], []
    for l in range(depth):
        kk_, vv_ = norm_proj(mem2, norm_mem[l], wkv_b[l], (D, D), (F32, F32), _token_tile(Bp * M, 512))
        pk.append(kk_.reshape(Bp, M, X_HEADS, X_HD))
        pv.append(vv_.reshape(Bp, M, X_HEADS, X_HD))
    p_mem_k = jnp.stack(pk)
    p_mem_v = jnp.stack(pv)
    y_prompt, p_swa_k, p_swa_v, p_gla, p_rwkv, p_shift = run(x_prompt, p_mem_k, p_mem_v, None, None, None, None, None)
    y_sample, s_swa_k, s_swa_v, s_gla, s_rwkv, s_shift = run(
        x_sample, cache_mem_k, cache_mem_v, cache_swa_k, cache_swa_v, state_gla, state_rwkv, state_shift)
    return (y_prompt, y_sample, p_swa_k, p_swa_v, p_gla, p_rwkv, p_shift, p_mem_k, p_mem_v,
            s_swa_k, s_swa_v, s_gla, s_rwkv, s_shift)
```
